```python
import math
import jax, jax.numpy as jnp
from jax import lax
import numpy as np

D_MODEL = 2048
BATCH = 8
SEQ = 4096
DEPTH = 1

EPS = 1e-6
CONV_WIDTH = D_MODEL // 2
CONV_K = 3
M_HEADS = 8
QK_DIM = D_MODEL // 16
V_DIM = D_MODEL // 8
M_QK = M_HEADS * QK_DIM
M_V = M_HEADS * V_DIM
CHUNK = 64
N_BRANCH = 2
FFN_HIDDEN = int(math.ceil((8 * D_MODEL / 3) / 256) * 256)
SPLITS = (CONV_WIDTH, CONV_WIDTH, CONV_WIDTH, M_QK, M_QK, M_V, M_V, 4 * M_HEADS, N_BRANCH * D_MODEL)
IN_COLS = sum(SPLITS)
SPLIT_IDX = tuple(int(v) for v in np.cumsum(SPLITS)[:-1])

kernel_name = "hybrid_conv_mlstm_gated_merge_encoder"


def rmsnorm(x, w):
    xf = x.astype(jnp.float32)
    y = xf * lax.rsqrt(jnp.mean(xf * xf, axis=-1, keepdims=True) + EPS)
    return (y * w.astype(jnp.float32)).astype(x.dtype)


def modulate(h, shift, scale):
    return h * (1.0 + scale[:, None, :]) + shift[:, None, :]


def short_conv_centred(u, w):
    p = jnp.pad(u, ((0, 0), (1, 1), (0, 0)))
    return w[0] * p[:, :-2] + w[1] * p[:, 1:-1] + w[2] * p[:, 2:]


def mlstm_chunk_step(carry, inp):
    C, n, m = carry
    q, k, v, ig, lf = inp
    L = q.shape[-2]
    tril = jnp.tril(jnp.ones((L, L), dtype=bool))
    b = jnp.cumsum(lf, axis=-1)
    d_log = b[..., :, None] - b[..., None, :] + ig[..., None, :]
    d_log = jnp.where(tril, d_log, -jnp.inf)
    m_inter = b + m[..., None]
    m_t = jnp.maximum(m_inter, jnp.max(d_log, axis=-1))
    s = jnp.einsum('bhjd,bhsd->bhjs', q, k) * jnp.exp(d_log - m_t[..., None])
    inter = jnp.exp(m_inter - m_t)
    num = jnp.einsum('bhjs,bhsv->bhjv', s, v) + inter[..., None] * jnp.einsum('bhjd,bhdv->bhjv', q, C)
    den = jnp.sum(s, axis=-1) + inter * jnp.einsum('bhjd,bhd->bhj', q, n)
    h = num / jnp.maximum(jnp.abs(den), jnp.exp(-m_t))[..., None]
    b_last = b[..., -1]
    w_log = b_last[..., None] - b + ig
    m_new = jnp.maximum(b_last + m, jnp.max(w_log, axis=-1))
    w = jnp.exp(w_log - m_new[..., None])
    decay = jnp.exp(b_last + m - m_new)
    C_new = decay[..., None, None] * C + jnp.einsum('bhs,bhsd,bhsv->bhdv', w, k, v)
    n_new = decay[..., None] * n + jnp.einsum('bhs,bhsd->bhd', w, k)
    return (C_new, n_new, m_new), h


def mlstm_scan(q, k, v, ig, lf):
    Bn, H, S, dk = q.shape
    dv = v.shape[-1]
    nc = S // CHUNK

    def chunks(t):
        return jnp.moveaxis(t.reshape(t.shape[:2] + (nc, CHUNK) + t.shape[3:]), 2, 0)

    init = (jnp.zeros((Bn, H, dk, dv), jnp.float32),
            jnp.zeros((Bn, H, dk), jnp.float32),
            jnp.zeros((Bn, H), jnp.float32))
    _, hs = lax.scan(mlstm_chunk_step, init, (chunks(q), chunks(k), chunks(v), chunks(ig), chunks(lf)))
    return jnp.moveaxis(hs, 0, 2).reshape(Bn, H, S, dv)


def to_heads(t, d):
    Bn, S, _ = t.shape
    return t.reshape(Bn, S, M_HEADS, d).transpose(0, 2, 1, 3).astype(jnp.float32)


def mixer(h, w_in_mix, conv_w, mlstm_gate_bias, mlstm_norm_w, w_conv_out, w_mlstm_out, w_o):
    Bn, S, _ = h.shape
    proj = jnp.einsum('bsd,de->bse', h, w_in_mix)
    cb, cc, cx, q, k, v, o, gpre, bgate = jnp.split(proj, SPLIT_IDX, axis=-1)

    y_conv = jnp.einsum('bsc,cd->bsd', cb * short_conv_centred(cc * cx, conv_w), w_conv_out)

    qh = to_heads(q, QK_DIM)
    kh = to_heads(k, QK_DIM) * (QK_DIM ** -0.5)
    vh = to_heads(v, V_DIM)
    g = (gpre + mlstm_gate_bias).astype(jnp.float32).reshape(Bn, S, 4, M_HEADS).transpose(2, 0, 3, 1)
    i_fwd, f_fwd, i_bwd, f_bwd = g[0], g[1], g[2], g[3]
    h_fwd = mlstm_scan(qh, kh, vh, i_fwd, jax.nn.log_sigmoid(f_fwd))
    flip = lambda t: jnp.flip(t, axis=2)
    h_bwd = flip(mlstm_scan(flip(qh), flip(kh), flip(vh), flip(i_bwd), flip(jax.nn.log_sigmoid(f_bwd))))
    hm = h_fwd + h_bwd
    hm = hm * lax.rsqrt(jnp.mean(hm * hm, axis=-1, keepdims=True) + EPS)
    hm = hm * mlstm_norm_w.astype(jnp.float32).reshape(M_HEADS, 1, V_DIM)
    hm = hm.transpose(0, 2, 1, 3).reshape(Bn, S, M_V).astype(h.dtype)
    hm = jax.nn.sigmoid(o) * hm
    y_mlstm = jnp.einsum('bsv,vd->bsd', hm, w_mlstm_out)

    g_conv, g_mlstm = jnp.split(jax.nn.sigmoid(bgate), 2, axis=-1)
    merged = g_conv * y_conv + g_mlstm * y_mlstm
    return jnp.einsum('bsd,de->bse', merged, w_o)


def swiglu(h, w_gate_up, w_down):
    gu = jnp.einsum('bsd,df->bsf', h, w_gate_up)
    gt, up = jnp.split(gu, 2, axis=-1)
    return jnp.einsum('bsf,fd->bsd', jax.nn.silu(gt) * up, w_down)


def setup_inputs(seed: int = 0) -> dict:
    key = jax.random.key(seed)
    ks = jax.random.split(key, 16)
    f32 = jnp.float32
    L = DEPTH

    def nrm(k, shape, fan_in, mult=1.0):
        return jax.random.normal(k, shape, f32) * (mult * fan_in ** -0.5)

    kb1, kb2 = jax.random.split(ks[7])
    i_bias = 0.1 * jax.random.normal(kb1, (L, 4, M_HEADS), f32)
    f_bias = 3.0 + 3.0 * jax.random.uniform(kb2, (L, 4, M_HEADS), f32)
    is_forget = jnp.array([0.0, 1.0, 0.0, 1.0], f32)[None, :, None]
    gate_bias = (is_forget * f_bias + (1.0 - is_forget) * i_bias).reshape(L, 4 * M_HEADS)

    return {
        "x": jax.random.normal(ks[0], (BATCH, SEQ, D_MODEL), f32),
        "c": jax.random.normal(ks[1], (BATCH, D_MODEL), f32),
        "w_ada": nrm(ks[2], (L, D_MODEL, 6 * D_MODEL), D_MODEL),
        "b_ada": 0.02 * jax.random.normal(ks[3], (L, 6 * D_MODEL), f32),
        "norm1_w": 1.0 + 0.05 * jax.random.normal(ks[4], (L, D_MODEL), f32),
        "w_in_mix": nrm(ks[5], (L, D_MODEL, IN_COLS), D_MODEL),
        "conv_w": nrm(ks[6], (L, CONV_K, CONV_WIDTH), CONV_K),
        "mlstm_gate_bias": gate_bias,
        "mlstm_norm_w": 1.0 + 0.05 * jax.random.normal(ks[8], (L, M_V), f32),
        "w_conv_out": nrm(ks[9], (L, CONV_WIDTH, D_MODEL), CONV_WIDTH),
        "w_mlstm_out": nrm(ks[10], (L, M_V, D_MODEL), M_V),
        "w_o": nrm(ks[11], (L, D_MODEL, D_MODEL), D_MODEL),
        "norm2_w": 1.0 + 0.05 * jax.random.normal(ks[12], (L, D_MODEL), f32),
        "w_gate_up": nrm(ks[13], (L, D_MODEL, 2 * FFN_HIDDEN), D_MODEL),
        "w_down": nrm(ks[14], (L, FFN_HIDDEN, D_MODEL), FFN_HIDDEN),
        "final_norm_w": 1.0 + 0.05 * jax.random.normal(ks[15], (D_MODEL,), f32),
    }


def reference(x, c, w_ada, b_ada, norm1_w, w_in_mix, conv_w, mlstm_gate_bias, mlstm_norm_w,
              w_conv_out, w_mlstm_out, w_o, norm2_w, w_gate_up, w_down, final_norm_w):
    c_act = jax.nn.silu(c)
    for layer in range(DEPTH):
        ada = jnp.einsum('bd,de->be', c_act, w_ada[layer]) + b_ada[layer]
        shift1, scale1, gate1, shift2, scale2, gate2 = jnp.split(ada, 6, axis=-1)
        h = modulate(rmsnorm(x, norm1_w[layer]), shift1, scale1)
        mix = mixer(h, w_in_mix[layer], conv_w[layer], mlstm_gate_bias[layer], mlstm_norm_w[layer],
                    w_conv_out[layer], w_mlstm_out[layer], w_o[layer])
        x = x + gate1[:, None, :] * mix
        h = modulate(rmsnorm(x, norm2_w[layer]), shift2, scale2)
        x = x + gate2[:, None, :] * swiglu(h, w_gate_up[layer], w_down[layer])
    return rmsnorm(x, final_norm_w)
```

```python
import functools
import math

import jax
import jax.numpy as jnp
from jax import lax
from jax.experimental import pallas as pl
from jax.experimental.pallas import tpu as pltpu

D_MODEL = 2048
BATCH = 8
SEQ = 4096
TOKENS = BATCH * SEQ
EPS = 1e-6
CONV_WIDTH = D_MODEL // 2
M_HEADS = 8
QK_DIM = D_MODEL // 16
V_DIM = D_MODEL // 8
M_QK = M_HEADS * QK_DIM
M_V = M_HEADS * V_DIM
N_GATES = 4 * M_HEADS
FFN_HIDDEN = int(math.ceil((8 * D_MODEL / 3) / 256) * 256)

OFF_CB, OFF_CC, OFF_CX = 0, CONV_WIDTH, 2 * CONV_WIDTH
OFF_Q = 3 * CONV_WIDTH
OFF_K = OFF_Q + M_QK
OFF_V = OFF_K + M_QK
OFF_O = OFF_V + M_V
OFF_G = OFF_O + M_V
OFF_BG = OFF_G + N_GATES
IN_COLS = OFF_BG + 2 * D_MODEL

MAIN_COLS = IN_COLS - M_QK - N_GATES
MAIN_Q = 3 * CONV_WIDTH
MAIN_V = MAIN_Q + M_QK
MAIN_O = MAIN_V + M_V
MAIN_GC = MAIN_O + M_V
MAIN_GM = MAIN_GC + D_MODEL

LANES = 128
BF16_SUBLANES = 16
VMEM_LIMIT = 56 * 1024 * 1024

CHUNK = 256
N_CHUNKS = SEQ // CHUNK
V_EXT = V_DIM + LANES

TM_IN = 1024
TN_IN = 512
NJ_MAIN = MAIN_COLS // TN_IN
NJ_K = M_QK // TN_IN
TM_MIX = 256
TM_FFN = 512
TF_FFN = 512
NF_FFN = FFN_HIDDEN // TF_FFN
TN_ADA = 1024
ROW_BLOCK = 128

F32 = jnp.float32
BF16 = jnp.bfloat16


def _dot(a, b):
    return jnp.dot(a, b, preferred_element_type=F32)


def _sigmoid(x):
    return 1.0 / (1.0 + jnp.exp(-x))


def _rms(x, norm_w):
    return x * lax.rsqrt(jnp.mean(x * x, axis=-1, keepdims=True) + EPS) * norm_w


def _row_blocks(n_rows, body):
    def step(r, carry):
        body(pl.ds(pl.multiple_of(r * ROW_BLOCK, ROW_BLOCK), ROW_BLOCK))
        return carry
    lax.fori_loop(0, n_rows // ROW_BLOCK, step, 0)


def _rms_modulate_into(dst_ref, x_ref, norm_w, scale, shift):
    def body(rows):
        y = _rms(x_ref[rows, :], norm_w)
        dst_ref[rows, :] = (y * (1.0 + scale) + shift).astype(BF16)
    _row_blocks(x_ref.shape[0], body)


def _ada_kernel(c_ref, w_ref, b_ref, o_ref):
    c = c_ref[...]
    c_act = (c * _sigmoid(c)).astype(BF16)
    o_ref[...] = _dot(c_act, w_ref[...].astype(BF16)) + b_ref[...]


def _ada_call(c, w_ada, b_ada):
    n = w_ada.shape[1]
    return pl.pallas_call(
        _ada_kernel,
        grid=(n // TN_ADA,),
        in_specs=[
            pl.BlockSpec((BATCH, D_MODEL), lambda j: (0, 0)),
            pl.BlockSpec((D_MODEL, TN_ADA), lambda j: (0, j)),
            pl.BlockSpec((1, TN_ADA), lambda j: (0, j)),
        ],
        out_specs=pl.BlockSpec((BATCH, TN_ADA), lambda j: (0, j)),
        out_shape=jax.ShapeDtypeStruct((BATCH, n), F32),
        compiler_params=pltpu.CompilerParams(
            dimension_semantics=("arbitrary",), vmem_limit_bytes=VMEM_LIMIT),
        name="ada",
    )(c, w_ada, b_ada.reshape(1, n))


def _inproj_kernel(x_ref, shift_ref, scale_ref, nw_ref, wmain_ref, wkt_ref, wg_ref, bg_ref,
                   main_ref, kt_ref, g_ref, h_scr):
    j = pl.program_id(1)

    @pl.when(j == 0)
    def _():
        _rms_modulate_into(h_scr, x_ref, nw_ref[...], scale_ref[0], shift_ref[0])
        g_ref[...] = _dot(h_scr[...], wg_ref[...]) + bg_ref[...]

    @pl.when(j < NJ_MAIN)
    def _():
        main_ref[...] = _dot(h_scr[...], wmain_ref[...]).astype(BF16)

    @pl.when(j >= NJ_MAIN)
    def _():
        kt = lax.dot_general(wkt_ref[...], h_scr[...], (((1,), (1,)), ((), ())),
                             preferred_element_type=F32)
        kt = (kt * (QK_DIM ** -0.5)).astype(BF16)
        for p in range(TM_IN // CHUNK):
            kt_ref[p] = kt[:, p * CHUNK:(p + 1) * CHUNK]


def _inproj_call(x2d, ada3, norm_w, w_main, w_kt, w_gate, b_gate):
    tiles_per_seq = SEQ // TM_IN
    chunks_per_tile = TM_IN // CHUNK

    def ada_map(k):
        return lambda i, j: ((i // tiles_per_seq) * 6 + k, 0, 0)

    return pl.pallas_call(
        _inproj_kernel,
        grid=(TOKENS // TM_IN, NJ_MAIN + NJ_K),
        in_specs=[
            pl.BlockSpec((TM_IN, D_MODEL), lambda i, j: (i, 0)),
            pl.BlockSpec((1, 1, D_MODEL), ada_map(0)),
            pl.BlockSpec((1, 1, D_MODEL), ada_map(1)),
            pl.BlockSpec((1, D_MODEL), lambda i, j: (0, 0)),
            pl.BlockSpec((D_MODEL, TN_IN), lambda i, j: (0, jnp.minimum(j, NJ_MAIN - 1))),
            pl.BlockSpec((TN_IN, D_MODEL), lambda i, j: (jnp.maximum(j - NJ_MAIN, 0), 0)),
            pl.BlockSpec((D_MODEL, LANES), lambda i, j: (0, 0)),
            pl.BlockSpec((1, LANES), lambda i, j: (0, 0)),
        ],
        out_specs=[
            pl.BlockSpec((TM_IN, TN_IN), lambda i, j: (i, jnp.minimum(j, NJ_MAIN - 1))),
            pl.BlockSpec((chunks_per_tile, TN_IN, CHUNK),
                         lambda i, j: (i, jnp.maximum(j - NJ_MAIN, 0), 0)),
            pl.BlockSpec((TM_IN, LANES), lambda i, j: (i, 0)),
        ],
        out_shape=[
            jax.ShapeDtypeStruct((TOKENS, MAIN_COLS), BF16),
            jax.ShapeDtypeStruct((TOKENS // CHUNK, M_QK, CHUNK), BF16),
            jax.ShapeDtypeStruct((TOKENS, LANES), F32),
        ],
        scratch_shapes=[pltpu.VMEM((TM_IN, D_MODEL), BF16)],
        compiler_params=pltpu.CompilerParams(
            dimension_semantics=("arbitrary", "arbitrary"), vmem_limit_bytes=VMEM_LIMIT),
        name="inproj",
    )(x2d, ada3, ada3, norm_w, w_main, w_kt, w_gate, b_gate)


def _log_sigmoid(x):
    return jnp.minimum(x, 0.0) - jnp.log1p(jnp.exp(-jnp.abs(x)))


def _mlstm_kernel(g_ref, q_ref, kt_ref, v_ref, o_ref, nw_ref, out_ref,
                  rows_scr, h_scr, cf_scr, cb_scr):
    L = CHUNK
    g = g_ref[0, 0]
    row_i = lax.broadcasted_iota(jnp.int32, (L, L), 0)
    col_i = lax.broadcasted_iota(jnp.int32, (L, L), 1)
    lower = col_i <= row_i
    upper = col_i >= row_i

    lf_f = _log_sigmoid(g[1])
    lf_b = _log_sigmoid(g[3])
    cum_f = jnp.dot(lf_f, upper.astype(F32), precision=lax.Precision.HIGHEST,
                    preferred_element_type=F32)
    cum_b = jnp.dot(lf_b, lower.astype(F32), precision=lax.Precision.HIGHEST,
                    preferred_element_type=F32)
    rows_scr[0] = lf_f
    rows_scr[1] = g[0] - cum_f
    rows_scr[2] = cum_f
    rows_scr[3] = lf_b
    rows_scr[4] = g[2] - cum_b
    rows_scr[5] = cum_b

    cf_scr[...] = jnp.zeros_like(cf_scr)
    cb_scr[...] = jnp.zeros_like(cb_scr)

    ones_block = (lax.broadcasted_iota(jnp.int32, (L, LANES), 1) == 0).astype(BF16)
    nw = nw_ref[...]

    def chunk(c, tri, base, total_lane, c_scr, m):
        lf_row = rows_scr[base, pl.ds(c, 1), :]
        a_row = rows_scr[base + 1, pl.ds(c, 1), :]
        cum_row = rows_scr[base + 2, pl.ds(c, 1), :]
        total = cum_row[:, total_lane:total_lane + 1]
        r0 = pl.multiple_of(c * L, L)
        q = q_ref[0, pl.ds(r0, L), :]
        kt = kt_ref[c]
        v_ext = jnp.concatenate([v_ref[0, pl.ds(r0, L), :], ones_block], axis=1)

        cum_col = jnp.sum(jnp.where(tri, lf_row, 0.0), axis=1, keepdims=True)
        d_log = jnp.where(tri, cum_col + a_row, -jnp.inf)
        m_inter = cum_col + m
        m_t = jnp.maximum(m_inter, jnp.max(d_log, axis=1, keepdims=True))
        p = jnp.exp(d_log - m_t)
        s = (_dot(q, kt) * p).astype(BF16)
        inter = jnp.exp(m_inter - m_t)
        c_state = c_scr[...]
        num = _dot(s, v_ext) + inter * _dot(q, c_state.astype(BF16))
        den = num[:, V_DIM:V_DIM + 1]
        h = num[:, :V_DIM] * (1.0 / jnp.maximum(jnp.abs(den), jnp.exp(-m_t)))

        w_log = total + a_row
        m_new = jnp.maximum(total + m, jnp.max(w_log, axis=1, keepdims=True))
        w = jnp.exp(w_log - m_new)
        decay = jnp.exp(total + m - m_new)
        ktw = (kt.astype(F32) * w).astype(BF16)
        c_scr[...] = decay * c_state + _dot(ktw, v_ext)
        return h, m_new

    def finish(c, h_sum):
        r0 = pl.multiple_of(c * L, L)
        hn = h_sum * lax.rsqrt(jnp.mean(h_sum * h_sum, axis=-1, keepdims=True) + EPS) * nw
        o = o_ref[0, pl.ds(r0, L), :].astype(F32)
        out_ref[0, pl.ds(r0, L), :] = (_sigmoid(o) * hn).astype(BF16)

    def step(i, carry, finalize):
        m_f, m_b = carry
        cf = i
        cb = N_CHUNKS - 1 - i
        h_f, m_f = chunk(cf, lower, 0, L - 1, cf_scr, m_f)
        h_b, m_b = chunk(cb, upper, 3, 0, cb_scr, m_b)
        rf = pl.multiple_of(cf * L, L)
        rb = pl.multiple_of(cb * L, L)
        if finalize:
            finish(cf, h_f + h_scr[pl.ds(rf, L), :])
            finish(cb, h_b + h_scr[pl.ds(rb, L), :])
        else:
            h_scr[pl.ds(rf, L), :] = h_f
            h_scr[pl.ds(rb, L), :] = h_b
        return m_f, m_b

    m0 = jnp.zeros((1, 1), F32)
    carry = lax.fori_loop(0, N_CHUNKS // 2, functools.partial(step, finalize=False), (m0, m0))
    lax.fori_loop(N_CHUNKS // 2, N_CHUNKS, functools.partial(step, finalize=True), carry)


def _mlstm_call(gates, main3, kt, norm_w_heads):
    q_blk = MAIN_Q // QK_DIM
    v_blk = MAIN_V // V_DIM
    o_blk = MAIN_O // V_DIM
    return pl.pallas_call(
        _mlstm_kernel,
        grid=(BATCH, M_HEADS),
        in_specs=[
            pl.BlockSpec((1, 1, 4, N_CHUNKS, CHUNK), lambda b, h: (b, h, 0, 0, 0)),
            pl.BlockSpec((1, SEQ, QK_DIM), lambda b, h: (b, 0, q_blk + h)),
            pl.BlockSpec((N_CHUNKS, QK_DIM, CHUNK), lambda b, h: (b, h, 0)),
            pl.BlockSpec((1, SEQ, V_DIM), lambda b, h: (b, 0, v_blk + h)),
            pl.BlockSpec((1, SEQ, V_DIM), lambda b, h: (b, 0, o_blk + h)),
            pl.BlockSpec((1, V_DIM), lambda b, h: (0, h)),
        ],
        out_specs=pl.BlockSpec((1, SEQ, V_DIM), lambda b, h: (b, 0, h)),
        out_shape=jax.ShapeDtypeStruct((BATCH, SEQ, M_V), BF16),
        scratch_shapes=[
            pltpu.VMEM((6, N_CHUNKS, CHUNK), F32),
            pltpu.VMEM((SEQ, V_DIM), F32),
            pltpu.VMEM((QK_DIM, V_EXT), F32),
            pltpu.VMEM((QK_DIM, V_EXT), F32),
        ],
        compiler_params=pltpu.CompilerParams(
            dimension_semantics=("arbitrary", "arbitrary"), vmem_limit_bytes=VMEM_LIMIT),
        name="mlstm",
    )(gates, main3, kt, main3, main3, norm_w_heads)


def _mix_kernel(x_ref, cb_ref, cc_ref, cx_ref, ccp_ref, cxp_ref, ccn_ref, cxn_ref,
                gc_ref, gm_ref, hm_ref, gate1_ref, convw_ref, wconv_ref, wml_ref, wo_ref,
                x1_ref):
    i = pl.program_id(0)
    tm = TM_MIX
    tiles_per_seq = SEQ // tm
    pos = i % tiles_per_seq
    u = cc_ref[...].astype(F32) * cx_ref[...].astype(F32)
    last = BF16_SUBLANES - 1
    u_prev = (ccp_ref[last:last + 1, :].astype(F32) * cxp_ref[last:last + 1, :].astype(F32))
    u_next = ccn_ref[0:1, :].astype(F32) * cxn_ref[0:1, :].astype(F32)
    u_prev = jnp.where(pos == 0, 0.0, u_prev)
    u_next = jnp.where(pos == tiles_per_seq - 1, 0.0, u_next)
    row = lax.broadcasted_iota(jnp.int32, (tm, 1), 0)
    u_m1 = jnp.where(row == 0, u_prev, pltpu.roll(u, 1, 0))
    u_p1 = jnp.where(row == tm - 1, u_next, pltpu.roll(u, tm - 1, 0))
    w = convw_ref[...]
    conv = w[0:1, :] * u_m1 + w[1:2, :] * u + w[2:3, :] * u_p1
    feat = (cb_ref[...].astype(F32) * conv).astype(BF16)
    y_conv = _dot(feat, wconv_ref[...])
    y_mlstm = _dot(hm_ref[...], wml_ref[...])
    merged = (_sigmoid(gc_ref[...].astype(F32)) * y_conv
              + _sigmoid(gm_ref[...].astype(F32)) * y_mlstm).astype(BF16)
    x1_ref[...] = x_ref[...] + gate1_ref[0] * _dot(merged, wo_ref[...])


def _mix_call(x2d, main, hm, ada3, conv_w, w_conv_out, w_mlstm_out, w_o):
    tm = TM_MIX
    tiles_per_seq = SEQ // tm
    halo = BF16_SUBLANES
    n_halo = TOKENS // halo
    per = tm // halo

    def col(k):
        return lambda i: (i, k)

    def prev_map(k):
        return lambda i: (jnp.maximum(i * per - 1, 0), k)

    def next_map(k):
        return lambda i: (jnp.minimum((i + 1) * per, n_halo - 1), k)

    def resident(shape):
        return pl.BlockSpec(shape, lambda i: (0, 0), pipeline_mode=pl.Buffered(1))

    cw = CONV_WIDTH
    return pl.pallas_call(
        _mix_kernel,
        grid=(TOKENS // tm,),
        in_specs=[
            pl.BlockSpec((tm, D_MODEL), lambda i: (i, 0)),
            pl.BlockSpec((tm, cw), col(0)),
            pl.BlockSpec((tm, cw), col(1)),
            pl.BlockSpec((tm, cw), col(2)),
            pl.BlockSpec((halo, cw), prev_map(1)),
            pl.BlockSpec((halo, cw), prev_map(2)),
            pl.BlockSpec((halo, cw), next_map(1)),
            pl.BlockSpec((halo, cw), next_map(2)),
            pl.BlockSpec((tm, D_MODEL), col(MAIN_GC // D_MODEL)),
            pl.BlockSpec((tm, D_MODEL), col(MAIN_GM // D_MODEL)),
            pl.BlockSpec((tm, M_V), lambda i: (i, 0)),
            pl.BlockSpec((1, 1, D_MODEL), lambda i: ((i // tiles_per_seq) * 6 + 2, 0, 0)),
            resident((3, cw)),
            resident((cw, D_MODEL)),
            resident((M_V, D_MODEL)),
            resident((D_MODEL, D_MODEL)),
        ],
        out_specs=pl.BlockSpec((tm, D_MODEL), lambda i: (i, 0)),
        out_shape=jax.ShapeDtypeStruct((TOKENS, D_MODEL), F32),
        compiler_params=pltpu.CompilerParams(
            dimension_semantics=("arbitrary",), vmem_limit_bytes=VMEM_LIMIT),
        name="mix",
    )(x2d, main, main, main, main, main, main, main, main, main, hm, ada3,
      conv_w, w_conv_out, w_mlstm_out, w_o)


def _ffn_kernel(x1_ref, shift_ref, scale_ref, gate_ref, nw_ref, fnw_ref, wg_ref, wu_ref, wd_ref,
                out_ref, h_scr, acc_scr):
    f = pl.program_id(1)

    @pl.when(f == 0)
    def _():
        _rms_modulate_into(h_scr, x1_ref, nw_ref[...], scale_ref[0], shift_ref[0])
        acc_scr[...] = jnp.zeros_like(acc_scr)

    h = h_scr[...]
    gt = _dot(h, wg_ref[...])
    up = _dot(h, wu_ref[...])
    act = (gt * _sigmoid(gt) * up).astype(BF16)
    acc_scr[...] += _dot(act, wd_ref[...])

    @pl.when(f == NF_FFN - 1)
    def _():
        gate = gate_ref[0]
        fnw = fnw_ref[...]

        def body(rows):
            out_ref[rows, :] = _rms(x1_ref[rows, :] + gate * acc_scr[rows, :], fnw)
        _row_blocks(TM_FFN, body)


def _ffn_call(x1, ada3, norm_w, final_norm_w, w_gate_up, w_down):
    tm = TM_FFN
    tiles_per_seq = SEQ // tm

    def ada_map(k):
        return lambda i, f: ((i // tiles_per_seq) * 6 + k, 0, 0)

    return pl.pallas_call(
        _ffn_kernel,
        grid=(TOKENS // tm, NF_FFN),
        in_specs=[
            pl.BlockSpec((tm, D_MODEL), lambda i, f: (i, 0)),
            pl.BlockSpec((1, 1, D_MODEL), ada_map(3)),
            pl.BlockSpec((1, 1, D_MODEL), ada_map(4)),
            pl.BlockSpec((1, 1, D_MODEL), ada_map(5)),
            pl.BlockSpec((1, D_MODEL), lambda i, f: (0, 0)),
            pl.BlockSpec((1, D_MODEL), lambda i, f: (0, 0)),
            pl.BlockSpec((D_MODEL, TF_FFN), lambda i, f: (0, f)),
            pl.BlockSpec((D_MODEL, TF_FFN), lambda i, f: (0, NF_FFN + f)),
            pl.BlockSpec((TF_FFN, D_MODEL), lambda i, f: (f, 0)),
        ],
        out_specs=pl.BlockSpec((tm, D_MODEL), lambda i, f: (i, 0)),
        out_shape=jax.ShapeDtypeStruct((TOKENS, D_MODEL), F32),
        scratch_shapes=[pltpu.VMEM((tm, D_MODEL), BF16), pltpu.VMEM((tm, D_MODEL), F32)],
        compiler_params=pltpu.CompilerParams(
            dimension_semantics=("arbitrary", "arbitrary"), vmem_limit_bytes=VMEM_LIMIT),
        name="ffn",
    )(x1, ada3, ada3, ada3, norm_w, final_norm_w, w_gate_up, w_gate_up, w_down)


def kernel(x, c, w_ada, b_ada, norm1_w, w_in_mix, conv_w, mlstm_gate_bias, mlstm_norm_w,
           w_conv_out, w_mlstm_out, w_o, norm2_w, w_gate_up, w_down, final_norm_w):
    assert x.shape == (BATCH, SEQ, D_MODEL) and w_ada.shape[0] == 1
    x2d = x.reshape(TOKENS, D_MODEL)

    ada = _ada_call(c, w_ada[0], b_ada[0])
    ada3 = ada.reshape(BATCH * 6, 1, D_MODEL)

    w_in = w_in_mix[0]
    w_main = jnp.concatenate([w_in[:, :OFF_K], w_in[:, OFF_V:OFF_G], w_in[:, OFF_BG:]],
                             axis=1).astype(BF16)
    w_kt = w_in[:, OFF_K:OFF_V].T.astype(BF16)
    w_gate = jnp.pad(w_in[:, OFF_G:OFF_BG], ((0, 0), (0, LANES - N_GATES))).astype(BF16)
    b_gate = jnp.pad(mlstm_gate_bias[0], (0, LANES - N_GATES)).reshape(1, LANES)

    main, kt, gpre = _inproj_call(x2d, ada3, norm1_w[0].reshape(1, D_MODEL),
                                  w_main, w_kt, w_gate, b_gate)

    gates = gpre[:, :N_GATES].reshape(BATCH, N_CHUNKS, CHUNK, 4, M_HEADS)
    gates = gates.transpose(0, 4, 3, 1, 2)
    hm = _mlstm_call(gates, main.reshape(BATCH, SEQ, MAIN_COLS), kt,
                     mlstm_norm_w[0].reshape(1, M_V))

    x1 = _mix_call(x2d, main, hm.reshape(TOKENS, M_V), ada3, conv_w[0],
                   w_conv_out[0].astype(BF16), w_mlstm_out[0].astype(BF16), w_o[0].astype(BF16))

    out = _ffn_call(x1, ada3, norm2_w[0].reshape(1, D_MODEL), final_norm_w.reshape(1, D_MODEL),
                    w_gate_up[0].astype(BF16), w_down[0].astype(BF16))
    return out.reshape(BATCH, SEQ, D_MODEL)
```

```python
import functools
import math

import jax
import jax.numpy as jnp
from jax import lax
from jax.experimental import pallas as pl
from jax.experimental.pallas import tpu as pltpu

D_MODEL = 2048
BATCH = 8
SEQ = 4096
TOKENS = BATCH * SEQ
EPS = 1e-6
CONV_WIDTH = D_MODEL // 2
M_HEADS = 8
QK_DIM = D_MODEL // 16
V_DIM = D_MODEL // 8
M_QK = M_HEADS * QK_DIM
M_V = M_HEADS * V_DIM
N_GATES = 4 * M_HEADS
FFN_HIDDEN = int(math.ceil((8 * D_MODEL / 3) / 256) * 256)

OFF_CB, OFF_CC, OFF_CX = 0, CONV_WIDTH, 2 * CONV_WIDTH
OFF_Q = 3 * CONV_WIDTH
OFF_K = OFF_Q + M_QK
OFF_V = OFF_K + M_QK
OFF_O = OFF_V + M_V
OFF_G = OFF_O + M_V
OFF_BG = OFF_G + N_GATES
IN_COLS = OFF_BG + 2 * D_MODEL

MAIN_COLS = IN_COLS - M_QK - N_GATES
MAIN_Q = 3 * CONV_WIDTH
MAIN_V = MAIN_Q + M_QK
MAIN_O = MAIN_V + M_V
MAIN_GC = MAIN_O + M_V
MAIN_GM = MAIN_GC + D_MODEL

LANES = 128
BF16_SUBLANES = 16
VMEM_LIMIT = 56 * 1024 * 1024

CHUNK = 256
N_CHUNKS = SEQ // CHUNK
MLSTM_UNROLL = 4
V_EXT = V_DIM + LANES

TM_IN = 1024
TN_IN = 512
NJ_A = OFF_G // TN_IN
J_K0 = OFF_K // TN_IN
NJ_K = M_QK // TN_IN
NJ_BG = 2 * D_MODEL // TN_IN
TM_MIX = 256
TM_FFN = 512
TF_FFN = 512
NF_FFN = FFN_HIDDEN // TF_FFN
TN_ADA = 1024
ROW_BLOCK = 128

F32 = jnp.float32
BF16 = jnp.bfloat16
LOG2E = math.log2(math.e)


def _dot(a, b):
    return jnp.dot(a, b, preferred_element_type=F32)


def _sigmoid(x):
    return 1.0 / (1.0 + jnp.exp(-x))


def _rms(x, norm_w):
    return x * lax.rsqrt(jnp.mean(x * x, axis=-1, keepdims=True) + EPS) * norm_w


def _row_blocks(n_rows, body):
    def step(r, carry):
        body(pl.ds(pl.multiple_of(r * ROW_BLOCK, ROW_BLOCK), ROW_BLOCK))
        return carry
    lax.fori_loop(0, n_rows // ROW_BLOCK, step, 0)


def _rms_modulate_into(dst_ref, x_ref, norm_w, scale, shift):
    gain = norm_w * (1.0 + scale)

    def body(rows):
        dst_ref[rows, :] = (_rms(x_ref[rows, :], gain) + shift).astype(BF16)
    _row_blocks(x_ref.shape[0], body)


def _ada_kernel(c_ref, w_ref, b_ref, o_ref):
    c = c_ref[...]
    c_act = (c * _sigmoid(c)).astype(BF16)
    o_ref[...] = _dot(c_act, w_ref[...].astype(BF16)) + b_ref[...]


def _ada_call(c, w_ada, b_ada):
    n = w_ada.shape[1]
    return pl.pallas_call(
        _ada_kernel,
        grid=(n // TN_ADA,),
        in_specs=[
            pl.BlockSpec((BATCH, D_MODEL), lambda j: (0, 0)),
            pl.BlockSpec((D_MODEL, TN_ADA), lambda j: (0, j)),
            pl.BlockSpec((1, TN_ADA), lambda j: (0, j)),
        ],
        out_specs=pl.BlockSpec((BATCH, TN_ADA), lambda j: (0, j)),
        out_shape=jax.ShapeDtypeStruct((BATCH, n), F32),
        compiler_params=pltpu.CompilerParams(
            dimension_semantics=("arbitrary",), vmem_limit_bytes=VMEM_LIMIT),
        name="ada",
    )(c, w_ada, b_ada.reshape(1, n))


def _inproj_kernel(x_ref, shift_ref, scale_ref, nw_ref, wa_ref, wbg_ref, wg_ref, bg_ref,
                   main_ref, kt_ref, g_ref, h_scr):
    j = pl.program_id(1)
    is_k = (j >= J_K0) & (j < J_K0 + NJ_K)

    @pl.when(j == 0)
    def _():
        _rms_modulate_into(h_scr, x_ref, nw_ref[...], scale_ref[0], shift_ref[0])
        g_ref[...] = _dot(h_scr[...], wg_ref[...]) + bg_ref[...]

    @pl.when((j < NJ_A) & jnp.logical_not(is_k))
    def _():
        main_ref[...] = _dot(h_scr[...], wa_ref[...]).astype(BF16)

    @pl.when(j >= NJ_A)
    def _():
        main_ref[...] = _dot(h_scr[...], wbg_ref[...]).astype(BF16)

    @pl.when(is_k)
    def _():
        kt = (_dot(h_scr[...], wa_ref[...]) * (QK_DIM ** -0.5)).T
        for p in range(TM_IN // CHUNK):
            kt_ref[p] = kt[:, p * CHUNK:(p + 1) * CHUNK].astype(BF16)


def _inproj_call(x2d, ada3, norm_w, w_a, w_bg, w_gate, b_gate):
    tiles_per_seq = SEQ // TM_IN
    chunks_per_tile = TM_IN // CHUNK

    def ada_map(k):
        return lambda i, j: ((i // tiles_per_seq) * 6 + k, 0, 0)

    def main_col(j):
        return jnp.where(j < J_K0, j, jnp.maximum(j - NJ_K, J_K0 - 1))

    return pl.pallas_call(
        _inproj_kernel,
        grid=(TOKENS // TM_IN, NJ_A + NJ_BG),
        in_specs=[
            pl.BlockSpec((TM_IN, D_MODEL), lambda i, j: (i, 0)),
            pl.BlockSpec((1, 1, D_MODEL), ada_map(0)),
            pl.BlockSpec((1, 1, D_MODEL), ada_map(1)),
            pl.BlockSpec((1, D_MODEL), lambda i, j: (0, 0)),
            pl.BlockSpec((D_MODEL, TN_IN), lambda i, j: (0, jnp.minimum(j, NJ_A - 1))),
            pl.BlockSpec((D_MODEL, TN_IN), lambda i, j: (0, jnp.maximum(j - NJ_A, 0))),
            pl.BlockSpec((D_MODEL, LANES), lambda i, j: (0, 0)),
            pl.BlockSpec((1, LANES), lambda i, j: (0, 0)),
        ],
        out_specs=[
            pl.BlockSpec((TM_IN, TN_IN), lambda i, j: (i, main_col(j))),
            pl.BlockSpec((chunks_per_tile, TN_IN, CHUNK),
                         lambda i, j: (i, jnp.clip(j - J_K0, 0, NJ_K - 1), 0)),
            pl.BlockSpec((TM_IN, LANES), lambda i, j: (i, 0)),
        ],
        out_shape=[
            jax.ShapeDtypeStruct((TOKENS, MAIN_COLS), BF16),
            jax.ShapeDtypeStruct((TOKENS // CHUNK, M_QK, CHUNK), BF16),
            jax.ShapeDtypeStruct((TOKENS, LANES), F32),
        ],
        scratch_shapes=[pltpu.VMEM((TM_IN, D_MODEL), BF16)],
        compiler_params=pltpu.CompilerParams(
            dimension_semantics=("arbitrary", "arbitrary"), vmem_limit_bytes=VMEM_LIMIT),
        name="inproj",
    )(x2d, ada3, ada3, norm_w, w_a, w_bg, w_gate, b_gate)


def _log_sigmoid(x):
    return jnp.minimum(x, 0.0) - jnp.log1p(jnp.exp(-jnp.abs(x)))


def _mlstm_kernel(g_ref, q_ref, kt_ref, v_ref, o_ref, nw_ref, out_ref,
                  rows_scr, h_scr, cf_scr, cb_scr):
    L = CHUNK
    g = g_ref[0, 0]
    row_i = lax.broadcasted_iota(jnp.int32, (L, L), 0)
    col_i = lax.broadcasted_iota(jnp.int32, (L, L), 1)
    lower = col_i <= row_i
    upper = col_i >= row_i

    lf_f = _log_sigmoid(g[1])
    lf_b = _log_sigmoid(g[3])
    cum_f = jnp.dot(lf_f, upper.astype(F32), precision=lax.Precision.HIGHEST,
                    preferred_element_type=F32)
    cum_b = jnp.dot(lf_b, lower.astype(F32), precision=lax.Precision.HIGHEST,
                    preferred_element_type=F32)
    a_f = (g[0] - cum_f) * LOG2E
    a_b = (g[2] - cum_b) * LOG2E
    full = (N_CHUNKS, L)
    rows_scr[0] = lf_f * LOG2E
    rows_scr[1] = a_f
    rows_scr[2] = jnp.broadcast_to(cum_f[:, L - 1:L] * LOG2E, full)
    rows_scr[3] = jnp.broadcast_to(jnp.max(a_f, axis=1, keepdims=True), full)
    rows_scr[4] = lf_b * LOG2E
    rows_scr[5] = a_b
    rows_scr[6] = jnp.broadcast_to(cum_b[:, 0:1] * LOG2E, full)
    rows_scr[7] = jnp.broadcast_to(jnp.max(a_b, axis=1, keepdims=True), full)

    cf_scr[...] = jnp.zeros_like(cf_scr)
    cb_scr[...] = jnp.zeros_like(cb_scr)

    ones_block = jnp.ones((L, LANES), BF16)
    nw = nw_ref[...]

    def chunk(c, tri, base, c_scr, m_row):
        lf_row = rows_scr[base, pl.ds(c, 1), :]
        a_row = rows_scr[base + 1, pl.ds(c, 1), :]
        tot_row = rows_scr[base + 2, pl.ds(c, 1), :]
        amax_row = rows_scr[base + 3, pl.ds(c, 1), :]
        r0 = pl.multiple_of(c * L, L)
        q = q_ref[0, pl.ds(r0, L), :]
        kt = kt_ref[c]
        v_ext = jnp.concatenate([v_ref[0, pl.ds(r0, L), :], ones_block], axis=1)

        cum_col = jnp.sum(jnp.where(tri, lf_row, 0.0), axis=1, keepdims=True)
        a_mat = jnp.where(tri, a_row, -jnp.inf)
        cm_col = jnp.max(a_mat, axis=1, keepdims=True)
        g_mat = jnp.maximum(cm_col, m_row)
        s = (_dot(q, kt) * jnp.exp2(a_mat - g_mat)).astype(BF16)
        g_blk = g_mat[:, :LANES]
        inter = jnp.exp2(m_row[:, :LANES] - g_blk)
        floor = jnp.exp2(-(cum_col + g_blk))
        q_inter = (q.astype(F32) * inter).astype(BF16)
        c_state = c_scr[...]
        num = _dot(s, v_ext) + _dot(q_inter, c_state.astype(BF16))
        den = num[:, V_DIM:]
        r = 1.0 / jnp.maximum(jnp.abs(den), floor)
        h = num[:, :V_DIM] * jnp.tile(r, (1, V_DIM // LANES))

        g_row = jnp.maximum(m_row, amax_row)
        w = jnp.exp2(a_row - g_row)
        decay = jnp.exp2(m_row - g_row)
        ktw = (kt.astype(F32) * w).astype(BF16)
        c_scr[...] = jnp.tile(decay[:, :LANES], (1, V_EXT // LANES)) * c_state + _dot(ktw, v_ext)
        return h, tot_row + g_row

    def finish(c, h_sum):
        r0 = pl.multiple_of(c * L, L)
        hn = h_sum * lax.rsqrt(jnp.mean(h_sum * h_sum, axis=-1, keepdims=True) + EPS) * nw
        out_ref[0, pl.ds(r0, L), :] = _sigmoid(o_ref[0, pl.ds(r0, L), :]) * hn.astype(BF16)

    def step(i, carry, finalize):
        m_f, m_b = carry
        cf = i
        cb = N_CHUNKS - 1 - i
        h_f, m_f = chunk(cf, lower, 0, cf_scr, m_f)
        h_b, m_b = chunk(cb, upper, 4, cb_scr, m_b)
        rf = pl.multiple_of(cf * L, L)
        rb = pl.multiple_of(cb * L, L)
        if finalize:
            finish(cf, h_f + h_scr[pl.ds(rf, L), :])
            finish(cb, h_b + h_scr[pl.ds(rb, L), :])
        else:
            h_scr[pl.ds(rf, L), :] = h_f
            h_scr[pl.ds(rb, L), :] = h_b
        return m_f, m_b

    m0 = jnp.zeros((1, L), F32)
    half = N_CHUNKS // 2
    carry = lax.fori_loop(0, half, functools.partial(step, finalize=False), (m0, m0),
                          unroll=MLSTM_UNROLL)
    lax.fori_loop(half, N_CHUNKS, functools.partial(step, finalize=True), carry,
                  unroll=MLSTM_UNROLL)


def _mlstm_call(gates, main3, kt, norm_w_heads):
    q_blk = MAIN_Q // QK_DIM
    v_blk = MAIN_V // V_DIM
    o_blk = MAIN_O // V_DIM
    return pl.pallas_call(
        _mlstm_kernel,
        grid=(BATCH, M_HEADS),
        in_specs=[
            pl.BlockSpec((1, 1, 4, N_CHUNKS, CHUNK), lambda b, h: (b, h, 0, 0, 0)),
            pl.BlockSpec((1, SEQ, QK_DIM), lambda b, h: (b, 0, q_blk + h)),
            pl.BlockSpec((N_CHUNKS, QK_DIM, CHUNK), lambda b, h: (b, h, 0)),
            pl.BlockSpec((1, SEQ, V_DIM), lambda b, h: (b, 0, v_blk + h)),
            pl.BlockSpec((1, SEQ, V_DIM), lambda b, h: (b, 0, o_blk + h)),
            pl.BlockSpec((1, V_DIM), lambda b, h: (0, h)),
        ],
        out_specs=pl.BlockSpec((1, SEQ, V_DIM), lambda b, h: (b, 0, h)),
        out_shape=jax.ShapeDtypeStruct((BATCH, SEQ, M_V), BF16),
        scratch_shapes=[
            pltpu.VMEM((8, N_CHUNKS, CHUNK), F32),
            pltpu.VMEM((SEQ, V_DIM), F32),
            pltpu.VMEM((QK_DIM, V_EXT), F32),
            pltpu.VMEM((QK_DIM, V_EXT), F32),
        ],
        compiler_params=pltpu.CompilerParams(
            dimension_semantics=("arbitrary", "arbitrary"), vmem_limit_bytes=VMEM_LIMIT),
        name="mlstm",
    )(gates, main3, kt, main3, main3, norm_w_heads)


def _mix_kernel(x_ref, cb_ref, cc_ref, cx_ref, ccp_ref, cxp_ref, ccn_ref, cxn_ref,
                gc_ref, gm_ref, hm_ref, gate1_ref, convw_ref, wconv_ref, wml_ref, wo_ref,
                x1_ref):
    i = pl.program_id(0)
    tm = TM_MIX
    tiles_per_seq = SEQ // tm
    pos = i % tiles_per_seq
    u = cc_ref[...].astype(F32) * cx_ref[...].astype(F32)
    last = BF16_SUBLANES - 1
    u_prev = (ccp_ref[last:last + 1, :].astype(F32) * cxp_ref[last:last + 1, :].astype(F32))
    u_next = ccn_ref[0:1, :].astype(F32) * cxn_ref[0:1, :].astype(F32)
    u_prev = jnp.where(pos == 0, 0.0, u_prev)
    u_next = jnp.where(pos == tiles_per_seq - 1, 0.0, u_next)
    row = lax.broadcasted_iota(jnp.int32, (tm, 1), 0)
    u_m1 = jnp.where(row == 0, u_prev, pltpu.roll(u, 1, 0))
    u_p1 = jnp.where(row == tm - 1, u_next, pltpu.roll(u, tm - 1, 0))
    w = convw_ref[...]
    conv = w[0:1, :] * u_m1 + w[1:2, :] * u + w[2:3, :] * u_p1
    feat = (cb_ref[...].astype(F32) * conv).astype(BF16)
    y_conv = _dot(feat, wconv_ref[...])
    y_mlstm = _dot(hm_ref[...], wml_ref[...])
    merged = (_sigmoid(gc_ref[...].astype(F32)) * y_conv
              + _sigmoid(gm_ref[...].astype(F32)) * y_mlstm).astype(BF16)
    x1_ref[...] = x_ref[...] + gate1_ref[0] * _dot(merged, wo_ref[...])


def _mix_call(x2d, main, hm, ada3, conv_w, w_conv_out, w_mlstm_out, w_o):
    tm = TM_MIX
    tiles_per_seq = SEQ // tm
    halo = BF16_SUBLANES
    n_halo = TOKENS // halo
    per = tm // halo

    def col(k):
        return lambda i: (i, k)

    def prev_map(k):
        return lambda i: (jnp.maximum(i * per - 1, 0), k)

    def next_map(k):
        return lambda i: (jnp.minimum((i + 1) * per, n_halo - 1), k)

    def resident(shape):
        return pl.BlockSpec(shape, lambda i: (0, 0), pipeline_mode=pl.Buffered(1))

    cw = CONV_WIDTH
    return pl.pallas_call(
        _mix_kernel,
        grid=(TOKENS // tm,),
        in_specs=[
            pl.BlockSpec((tm, D_MODEL), lambda i: (i, 0)),
            pl.BlockSpec((tm, cw), col(0)),
            pl.BlockSpec((tm, cw), col(1)),
            pl.BlockSpec((tm, cw), col(2)),
            pl.BlockSpec((halo, cw), prev_map(1)),
            pl.BlockSpec((halo, cw), prev_map(2)),
            pl.BlockSpec((halo, cw), next_map(1)),
            pl.BlockSpec((halo, cw), next_map(2)),
            pl.BlockSpec((tm, D_MODEL), col(MAIN_GC // D_MODEL)),
            pl.BlockSpec((tm, D_MODEL), col(MAIN_GM // D_MODEL)),
            pl.BlockSpec((tm, M_V), lambda i: (i, 0)),
            pl.BlockSpec((1, 1, D_MODEL), lambda i: ((i // tiles_per_seq) * 6 + 2, 0, 0)),
            resident((3, cw)),
            resident((cw, D_MODEL)),
            resident((M_V, D_MODEL)),
            resident((D_MODEL, D_MODEL)),
        ],
        out_specs=pl.BlockSpec((tm, D_MODEL), lambda i: (i, 0)),
        out_shape=jax.ShapeDtypeStruct((TOKENS, D_MODEL), F32),
        compiler_params=pltpu.CompilerParams(
            dimension_semantics=("arbitrary",), vmem_limit_bytes=VMEM_LIMIT),
        name="mix",
    )(x2d, main, main, main, main, main, main, main, main, main, hm, ada3,
      conv_w, w_conv_out, w_mlstm_out, w_o)


def _ffn_kernel(x1_ref, shift_ref, scale_ref, gate_ref, nw_ref, fnw_ref, wg_ref, wu_ref, wd_ref,
                out_ref, h_scr, acc_scr):
    f = pl.program_id(1)

    @pl.when(f == 0)
    def _():
        _rms_modulate_into(h_scr, x1_ref, nw_ref[...], scale_ref[0], shift_ref[0])
        acc_scr[...] = jnp.zeros_like(acc_scr)

    h = h_scr[...]
    gt = _dot(h, wg_ref[...])
    up = _dot(h, wu_ref[...])
    act = (gt * _sigmoid(gt) * up).astype(BF16)
    acc_scr[...] += _dot(act, wd_ref[...])

    @pl.when(f == NF_FFN - 1)
    def _():
        gate = gate_ref[0]
        fnw = fnw_ref[...]

        def body(rows):
            out_ref[rows, :] = _rms(x1_ref[rows, :] + gate * acc_scr[rows, :], fnw)
        _row_blocks(TM_FFN, body)


def _ffn_call(x1, ada3, norm_w, final_norm_w, w_gate_up, w_down):
    tm = TM_FFN
    tiles_per_seq = SEQ // tm

    def ada_map(k):
        return lambda i, f: ((i // tiles_per_seq) * 6 + k, 0, 0)

    return pl.pallas_call(
        _ffn_kernel,
        grid=(TOKENS // tm, NF_FFN),
        in_specs=[
            pl.BlockSpec((tm, D_MODEL), lambda i, f: (i, 0)),
            pl.BlockSpec((1, 1, D_MODEL), ada_map(3)),
            pl.BlockSpec((1, 1, D_MODEL), ada_map(4)),
            pl.BlockSpec((1, 1, D_MODEL), ada_map(5)),
            pl.BlockSpec((1, D_MODEL), lambda i, f: (0, 0)),
            pl.BlockSpec((1, D_MODEL), lambda i, f: (0, 0)),
            pl.BlockSpec((D_MODEL, TF_FFN), lambda i, f: (0, f)),
            pl.BlockSpec((D_MODEL, TF_FFN), lambda i, f: (0, NF_FFN + f)),
            pl.BlockSpec((TF_FFN, D_MODEL), lambda i, f: (f, 0)),
        ],
        out_specs=pl.BlockSpec((tm, D_MODEL), lambda i, f: (i, 0)),
        out_shape=jax.ShapeDtypeStruct((TOKENS, D_MODEL), F32),
        scratch_shapes=[pltpu.VMEM((tm, D_MODEL), BF16), pltpu.VMEM((tm, D_MODEL), F32)],
        compiler_params=pltpu.CompilerParams(
            dimension_semantics=("arbitrary", "arbitrary"), vmem_limit_bytes=VMEM_LIMIT),
        name="ffn",
    )(x1, ada3, ada3, ada3, norm_w, final_norm_w, w_gate_up, w_gate_up, w_down)


def kernel(x, c, w_ada, b_ada, norm1_w, w_in_mix, conv_w, mlstm_gate_bias, mlstm_norm_w,
           w_conv_out, w_mlstm_out, w_o, norm2_w, w_gate_up, w_down, final_norm_w):
    assert x.shape == (BATCH, SEQ, D_MODEL) and w_ada.shape[0] == 1
    x2d = x.reshape(TOKENS, D_MODEL)

    ada = _ada_call(c, w_ada[0], b_ada[0])
    ada3 = ada.reshape(BATCH * 6, 1, D_MODEL)

    w_in = w_in_mix[0]
    w_a = w_in.astype(BF16)
    w_bg = w_in[:, OFF_BG:].astype(BF16)
    w_gate = jnp.pad(w_in[:, OFF_G:OFF_BG], ((0, 0), (0, LANES - N_GATES))).astype(BF16)
    b_gate = jnp.pad(mlstm_gate_bias[0], (0, LANES - N_GATES)).reshape(1, LANES)

    main, kt, gpre = _inproj_call(x2d, ada3, norm1_w[0].reshape(1, D_MODEL),
                                  w_a, w_bg, w_gate, b_gate)

    gates = gpre[:, :N_GATES].reshape(BATCH, N_CHUNKS, CHUNK, 4, M_HEADS)
    gates = gates.transpose(0, 4, 3, 1, 2)
    hm = _mlstm_call(gates, main.reshape(BATCH, SEQ, MAIN_COLS), kt,
                     mlstm_norm_w[0].reshape(1, M_V))

    x1 = _mix_call(x2d, main, hm.reshape(TOKENS, M_V), ada3, conv_w[0],
                   w_conv_out[0].astype(BF16), w_mlstm_out[0].astype(BF16), w_o[0].astype(BF16))

    out = _ffn_call(x1, ada3, norm2_w[0].reshape(1, D_MODEL), final_norm_w.reshape(1, D_MODEL),
                    w_gate_up[0].astype(BF16), w_down[0].astype(BF16))
    return out.reshape(BATCH, SEQ, D_MODEL)
```

```python
import functools
import math

import jax
import jax.numpy as jnp
from jax import lax
from jax.experimental import pallas as pl
from jax.experimental.pallas import tpu as pltpu

D_MODEL = 2048
BATCH = 8
SEQ = 4096
TOKENS = BATCH * SEQ
EPS = 1e-6
CONV_WIDTH = D_MODEL // 2
M_HEADS = 8
QK_DIM = D_MODEL // 16
V_DIM = D_MODEL // 8
M_QK = M_HEADS * QK_DIM
M_V = M_HEADS * V_DIM
N_GATES = 4 * M_HEADS
FFN_HIDDEN = int(math.ceil((8 * D_MODEL / 3) / 256) * 256)

OFF_CB, OFF_CC, OFF_CX = 0, CONV_WIDTH, 2 * CONV_WIDTH
OFF_Q = 3 * CONV_WIDTH
OFF_K = OFF_Q + M_QK
OFF_V = OFF_K + M_QK
OFF_O = OFF_V + M_V
OFF_G = OFF_O + M_V
OFF_BG = OFF_G + N_GATES
IN_COLS = OFF_BG + 2 * D_MODEL

LANES = 128
BF16_SUBLANES = 16
VMEM_LIMIT = 56 * 1024 * 1024

CHUNK = 256
N_CHUNKS = SEQ // CHUNK
MLSTM_UNROLL = 4
V_EXT = V_DIM + LANES

TM_IN = 1024
TN_IN = 512
J_CONV = (0, OFF_Q // TN_IN)
J_Q = (J_CONV[1], OFF_K // TN_IN)
J_K = (J_Q[1], OFF_V // TN_IN)
J_V = (J_K[1], OFF_O // TN_IN)
J_O = (J_V[1], OFF_G // TN_IN)
NJ_A = J_O[1]
NJ_BG = 2 * D_MODEL // TN_IN
Q_PER_TILE = TN_IN // QK_DIM
V_PER_TILE = TN_IN // V_DIM
TM_MIX = 256
TM_FFN = 512
TF_FFN = 512
NF_FFN = FFN_HIDDEN // TF_FFN
TN_ADA = 1024
ROW_BLOCK = 128

F32 = jnp.float32
BF16 = jnp.bfloat16
LOG2E = math.log2(math.e)


def _dot(a, b):
    return jnp.dot(a, b, preferred_element_type=F32)


def _sigmoid(x):
    return 1.0 / (1.0 + jnp.exp(-x))


def _rms(x, norm_w):
    return x * lax.rsqrt(jnp.mean(x * x, axis=-1, keepdims=True) + EPS) * norm_w


def _row_blocks(n_rows, body):
    def step(r, carry):
        body(pl.ds(pl.multiple_of(r * ROW_BLOCK, ROW_BLOCK), ROW_BLOCK))
        return carry
    lax.fori_loop(0, n_rows // ROW_BLOCK, step, 0)


def _rms_modulate_into(dst_ref, x_ref, norm_w, scale, shift):
    gain = norm_w * (1.0 + scale)

    def body(rows):
        dst_ref[rows, :] = (_rms(x_ref[rows, :], gain) + shift).astype(BF16)
    _row_blocks(x_ref.shape[0], body)


def _ada_kernel(c_ref, w_ref, b_ref, o_ref):
    c = c_ref[...]
    c_act = (c * _sigmoid(c)).astype(BF16)
    o_ref[...] = _dot(c_act, w_ref[...].astype(BF16)) + b_ref[...]


def _ada_call(c, w_ada, b_ada):
    n = w_ada.shape[1]
    return pl.pallas_call(
        _ada_kernel,
        grid=(n // TN_ADA,),
        in_specs=[
            pl.BlockSpec((BATCH, D_MODEL), lambda j: (0, 0)),
            pl.BlockSpec((D_MODEL, TN_ADA), lambda j: (0, j)),
            pl.BlockSpec((1, TN_ADA), lambda j: (0, j)),
        ],
        out_specs=pl.BlockSpec((BATCH, TN_ADA), lambda j: (0, j)),
        out_shape=jax.ShapeDtypeStruct((BATCH, n), F32),
        compiler_params=pltpu.CompilerParams(
            dimension_semantics=("arbitrary",), vmem_limit_bytes=VMEM_LIMIT),
        name="ada",
    )(c, w_ada, b_ada.reshape(1, n))


def _inproj_kernel(x_ref, shift_ref, scale_ref, nw_ref, wa_ref, wbg_ref, wg_ref, bg_ref,
                   conv_ref, q_ref, kt_ref, v_ref, o_ref, bgo_ref, g_ref, h_scr):
    j = pl.program_id(1)

    def in_range(rng):
        return (j >= rng[0]) & (j < rng[1])

    def store_heads(dst_ref, res, width):
        for hh in range(TN_IN // width):
            dst_ref[hh] = res[:, hh * width:(hh + 1) * width].astype(BF16)

    @pl.when(j == 0)
    def _():
        _rms_modulate_into(h_scr, x_ref, nw_ref[...], scale_ref[0], shift_ref[0])
        g_ref[...] = _dot(h_scr[...], wg_ref[...]) + bg_ref[...]

    @pl.when(in_range(J_CONV))
    def _():
        conv_ref[0] = _dot(h_scr[...], wa_ref[...]).astype(BF16)

    @pl.when(in_range(J_Q))
    def _():
        store_heads(q_ref, _dot(h_scr[...], wa_ref[...]), QK_DIM)

    @pl.when(in_range(J_K))
    def _():
        kt = (_dot(h_scr[...], wa_ref[...]) * (QK_DIM ** -0.5)).T
        for p in range(TM_IN // CHUNK):
            kt_ref[p] = kt[:, p * CHUNK:(p + 1) * CHUNK].astype(BF16)

    @pl.when(in_range(J_V))
    def _():
        store_heads(v_ref, _dot(h_scr[...], wa_ref[...]), V_DIM)

    @pl.when(in_range(J_O))
    def _():
        store_heads(o_ref, _dot(h_scr[...], wa_ref[...]), V_DIM)

    @pl.when(j >= NJ_A)
    def _():
        bgo_ref[0] = _dot(h_scr[...], wbg_ref[...]).astype(BF16)


def _inproj_call(x2d, ada3, norm_w, w_a, w_bg, w_gate, b_gate):
    tiles_per_seq = SEQ // TM_IN
    chunks_per_tile = TM_IN // CHUNK

    def ada_map(k):
        return lambda i, j: ((i // tiles_per_seq) * 6 + k, 0, 0)

    def tile_in(rng):
        return lambda i, j: (jnp.clip(j - rng[0], 0, rng[1] - rng[0] - 1), i, 0)

    return pl.pallas_call(
        _inproj_kernel,
        grid=(TOKENS // TM_IN, NJ_A + NJ_BG),
        in_specs=[
            pl.BlockSpec((TM_IN, D_MODEL), lambda i, j: (i, 0)),
            pl.BlockSpec((1, 1, D_MODEL), ada_map(0)),
            pl.BlockSpec((1, 1, D_MODEL), ada_map(1)),
            pl.BlockSpec((1, D_MODEL), lambda i, j: (0, 0)),
            pl.BlockSpec((D_MODEL, TN_IN), lambda i, j: (0, jnp.minimum(j, NJ_A - 1))),
            pl.BlockSpec((D_MODEL, TN_IN), lambda i, j: (0, jnp.maximum(j - NJ_A, 0))),
            pl.BlockSpec((D_MODEL, LANES), lambda i, j: (0, 0)),
            pl.BlockSpec((1, LANES), lambda i, j: (0, 0)),
        ],
        out_specs=[
            pl.BlockSpec((1, TM_IN, TN_IN), tile_in(J_CONV)),
            pl.BlockSpec((Q_PER_TILE, TM_IN, QK_DIM), tile_in(J_Q)),
            pl.BlockSpec((chunks_per_tile, TN_IN, CHUNK),
                         lambda i, j: (i, jnp.clip(j - J_K[0], 0, J_K[1] - J_K[0] - 1), 0)),
            pl.BlockSpec((V_PER_TILE, TM_IN, V_DIM), tile_in(J_V)),
            pl.BlockSpec((V_PER_TILE, TM_IN, V_DIM), tile_in(J_O)),
            pl.BlockSpec((1, TM_IN, TN_IN), tile_in((NJ_A, NJ_A + NJ_BG))),
            pl.BlockSpec((TM_IN, LANES), lambda i, j: (i, 0)),
        ],
        out_shape=[
            jax.ShapeDtypeStruct((J_CONV[1] - J_CONV[0], TOKENS, TN_IN), BF16),
            jax.ShapeDtypeStruct((M_HEADS, TOKENS, QK_DIM), BF16),
            jax.ShapeDtypeStruct((TOKENS // CHUNK, M_QK, CHUNK), BF16),
            jax.ShapeDtypeStruct((M_HEADS, TOKENS, V_DIM), BF16),
            jax.ShapeDtypeStruct((M_HEADS, TOKENS, V_DIM), BF16),
            jax.ShapeDtypeStruct((NJ_BG, TOKENS, TN_IN), BF16),
            jax.ShapeDtypeStruct((TOKENS, LANES), F32),
        ],
        scratch_shapes=[pltpu.VMEM((TM_IN, D_MODEL), BF16)],
        compiler_params=pltpu.CompilerParams(
            dimension_semantics=("arbitrary", "arbitrary"), vmem_limit_bytes=VMEM_LIMIT),
        name="inproj",
    )(x2d, ada3, ada3, norm_w, w_a, w_bg, w_gate, b_gate)


def _log_sigmoid(x):
    return jnp.minimum(x, 0.0) - jnp.log1p(jnp.exp(-jnp.abs(x)))


def _mlstm_kernel(g_ref, q_ref, kt_ref, v_ref, o_ref, nw_ref, out_ref,
                  rows_scr, h_scr, cf_scr, cb_scr):
    L = CHUNK
    g = g_ref[0, 0]
    row_i = lax.broadcasted_iota(jnp.int32, (L, L), 0)
    col_i = lax.broadcasted_iota(jnp.int32, (L, L), 1)
    lower = col_i <= row_i
    upper = col_i >= row_i

    lf_f = _log_sigmoid(g[1])
    lf_b = _log_sigmoid(g[3])
    cum_f = jnp.dot(lf_f, upper.astype(F32), precision=lax.Precision.HIGHEST,
                    preferred_element_type=F32)
    cum_b = jnp.dot(lf_b, lower.astype(F32), precision=lax.Precision.HIGHEST,
                    preferred_element_type=F32)
    a_f = (g[0] - cum_f) * LOG2E
    a_b = (g[2] - cum_b) * LOG2E
    full = (N_CHUNKS, L)
    rows_scr[0] = lf_f * LOG2E
    rows_scr[1] = a_f
    rows_scr[2] = jnp.broadcast_to(cum_f[:, L - 1:L] * LOG2E, full)
    rows_scr[3] = jnp.broadcast_to(jnp.max(a_f, axis=1, keepdims=True), full)
    rows_scr[4] = lf_b * LOG2E
    rows_scr[5] = a_b
    rows_scr[6] = jnp.broadcast_to(cum_b[:, 0:1] * LOG2E, full)
    rows_scr[7] = jnp.broadcast_to(jnp.max(a_b, axis=1, keepdims=True), full)

    cf_scr[...] = jnp.zeros_like(cf_scr)
    cb_scr[...] = jnp.zeros_like(cb_scr)

    ones_block = jnp.ones((L, LANES), BF16)
    nw = nw_ref[...]

    def chunk(c, tri, base, c_scr, m_row):
        lf_row = rows_scr[base, pl.ds(c, 1), :]
        a_row = rows_scr[base + 1, pl.ds(c, 1), :]
        tot_row = rows_scr[base + 2, pl.ds(c, 1), :]
        amax_row = rows_scr[base + 3, pl.ds(c, 1), :]
        r0 = pl.multiple_of(c * L, L)
        q = q_ref[0, pl.ds(r0, L), :]
        kt = kt_ref[c]
        v_ext = jnp.concatenate([v_ref[0, pl.ds(r0, L), :], ones_block], axis=1)

        cum_col = jnp.sum(jnp.where(tri, lf_row, 0.0), axis=1, keepdims=True)
        a_mat = jnp.where(tri, a_row, -jnp.inf)
        cm_col = jnp.max(a_mat, axis=1, keepdims=True)
        g_mat = jnp.maximum(cm_col, m_row)
        s = (_dot(q, kt) * jnp.exp2(a_mat - g_mat)).astype(BF16)
        g_blk = g_mat[:, :LANES]
        inter = jnp.exp2(m_row[:, :LANES] - g_blk)
        floor = jnp.exp2(-(cum_col + g_blk))
        q_inter = (q.astype(F32) * inter).astype(BF16)
        c_state = c_scr[...]
        num = _dot(s, v_ext) + _dot(q_inter, c_state.astype(BF16))
        den = num[:, V_DIM:]
        r = 1.0 / jnp.maximum(jnp.abs(den), floor)
        h = num[:, :V_DIM] * jnp.tile(r, (1, V_DIM // LANES))

        g_row = jnp.maximum(m_row, amax_row)
        w = jnp.exp2(a_row - g_row)
        decay = jnp.exp2(m_row - g_row)
        ktw = (kt.astype(F32) * w).astype(BF16)
        c_scr[...] = jnp.tile(decay[:, :LANES], (1, V_EXT // LANES)) * c_state + _dot(ktw, v_ext)
        return h, tot_row + g_row

    def finish(c, h_sum):
        r0 = pl.multiple_of(c * L, L)
        hn = h_sum * lax.rsqrt(jnp.mean(h_sum * h_sum, axis=-1, keepdims=True) + EPS) * nw
        out_ref[0, pl.ds(r0, L), :] = _sigmoid(o_ref[0, pl.ds(r0, L), :]) * hn.astype(BF16)

    def step(i, carry, finalize):
        m_f, m_b = carry
        cf = i
        cb = N_CHUNKS - 1 - i
        h_f, m_f = chunk(cf, lower, 0, cf_scr, m_f)
        h_b, m_b = chunk(cb, upper, 4, cb_scr, m_b)
        rf = pl.multiple_of(cf * L, L)
        rb = pl.multiple_of(cb * L, L)
        if finalize:
            finish(cf, h_f + h_scr[pl.ds(rf, L), :])
            finish(cb, h_b + h_scr[pl.ds(rb, L), :])
        else:
            h_scr[pl.ds(rf, L), :] = h_f
            h_scr[pl.ds(rb, L), :] = h_b
        return m_f, m_b

    m0 = jnp.zeros((1, L), F32)
    half = N_CHUNKS // 2
    carry = lax.fori_loop(0, half, functools.partial(step, finalize=False), (m0, m0),
                          unroll=MLSTM_UNROLL)
    lax.fori_loop(half, N_CHUNKS, functools.partial(step, finalize=True), carry,
                  unroll=MLSTM_UNROLL)


def _mlstm_call(gates, q_hm, kt, v_hm, o_hm, norm_w_heads):
    def head_seq(dim):
        return pl.BlockSpec((1, SEQ, dim), lambda b, h: (h, b, 0))

    return pl.pallas_call(
        _mlstm_kernel,
        grid=(BATCH, M_HEADS),
        in_specs=[
            pl.BlockSpec((1, 1, 4, N_CHUNKS, CHUNK), lambda b, h: (b, h, 0, 0, 0)),
            head_seq(QK_DIM),
            pl.BlockSpec((N_CHUNKS, QK_DIM, CHUNK), lambda b, h: (b, h, 0)),
            head_seq(V_DIM),
            head_seq(V_DIM),
            pl.BlockSpec((1, V_DIM), lambda b, h: (0, h)),
        ],
        out_specs=head_seq(V_DIM),
        out_shape=jax.ShapeDtypeStruct((M_HEADS, TOKENS, V_DIM), BF16),
        scratch_shapes=[
            pltpu.VMEM((8, N_CHUNKS, CHUNK), F32),
            pltpu.VMEM((SEQ, V_DIM), F32),
            pltpu.VMEM((QK_DIM, V_EXT), F32),
            pltpu.VMEM((QK_DIM, V_EXT), F32),
        ],
        compiler_params=pltpu.CompilerParams(
            dimension_semantics=("arbitrary", "arbitrary"), vmem_limit_bytes=VMEM_LIMIT),
        name="mlstm",
    )(gates, q_hm, kt, v_hm, o_hm, norm_w_heads)


def _mix_kernel(x_ref, cb_ref, cc_ref, cx_ref, ccp_ref, cxp_ref, ccn_ref, cxn_ref,
                gc_ref, gm_ref, hm_ref, gate1_ref, convw_ref, wconv_ref, wml_ref, wo_ref,
                x1_ref):
    i = pl.program_id(0)
    tm = TM_MIX
    tiles_per_seq = SEQ // tm
    pos = i % tiles_per_seq

    def cat(ref, rows=slice(None)):
        return jnp.concatenate([ref[k, rows, :] for k in range(ref.shape[0])], axis=1).astype(F32)

    u = cat(cc_ref) * cat(cx_ref)
    last = slice(BF16_SUBLANES - 1, BF16_SUBLANES)
    u_prev = cat(ccp_ref, last) * cat(cxp_ref, last)
    u_next = cat(ccn_ref, slice(0, 1)) * cat(cxn_ref, slice(0, 1))
    u_prev = jnp.where(pos == 0, 0.0, u_prev)
    u_next = jnp.where(pos == tiles_per_seq - 1, 0.0, u_next)
    row = lax.broadcasted_iota(jnp.int32, (tm, 1), 0)
    u_m1 = jnp.where(row == 0, u_prev, pltpu.roll(u, 1, 0))
    u_p1 = jnp.where(row == tm - 1, u_next, pltpu.roll(u, tm - 1, 0))
    w = convw_ref[...]
    conv = w[0:1, :] * u_m1 + w[1:2, :] * u + w[2:3, :] * u_p1
    feat = (cat(cb_ref) * conv).astype(BF16)
    y_conv = _dot(feat, wconv_ref[...])
    hm = jnp.concatenate([hm_ref[h] for h in range(M_HEADS)], axis=1)
    y_mlstm = _dot(hm, wml_ref[...])
    merged = (_sigmoid(cat(gc_ref)) * y_conv + _sigmoid(cat(gm_ref)) * y_mlstm).astype(BF16)
    x1_ref[...] = x_ref[...] + gate1_ref[0] * _dot(merged, wo_ref[...])


def _mix_call(x2d, conv3, bgo, hm, ada3, conv_w, w_conv_out, w_mlstm_out, w_o):
    tm = TM_MIX
    tiles_per_seq = SEQ // tm
    halo = BF16_SUBLANES
    n_halo = TOKENS // halo
    per = tm // halo
    conv_tiles = CONV_WIDTH // TN_IN
    gate_tiles = D_MODEL // TN_IN

    def rows(n_tiles, k):
        return pl.BlockSpec((n_tiles, tm, TN_IN), lambda i: (k, i, 0))

    def prev_rows(k):
        return pl.BlockSpec((conv_tiles, halo, TN_IN),
                            lambda i: (k, jnp.maximum(i * per - 1, 0), 0))

    def next_rows(k):
        return pl.BlockSpec((conv_tiles, halo, TN_IN),
                            lambda i: (k, jnp.minimum((i + 1) * per, n_halo - 1), 0))

    def resident(shape):
        return pl.BlockSpec(shape, lambda i: (0, 0), pipeline_mode=pl.Buffered(1))

    cw = CONV_WIDTH
    return pl.pallas_call(
        _mix_kernel,
        grid=(TOKENS // tm,),
        in_specs=[
            pl.BlockSpec((tm, D_MODEL), lambda i: (i, 0)),
            rows(conv_tiles, 0),
            rows(conv_tiles, 1),
            rows(conv_tiles, 2),
            prev_rows(1),
            prev_rows(2),
            next_rows(1),
            next_rows(2),
            rows(gate_tiles, 0),
            rows(gate_tiles, 1),
            pl.BlockSpec((M_HEADS, tm, V_DIM), lambda i: (0, i, 0)),
            pl.BlockSpec((1, 1, D_MODEL), lambda i: ((i // tiles_per_seq) * 6 + 2, 0, 0)),
            resident((3, cw)),
            resident((cw, D_MODEL)),
            resident((M_V, D_MODEL)),
            resident((D_MODEL, D_MODEL)),
        ],
        out_specs=pl.BlockSpec((tm, D_MODEL), lambda i: (i, 0)),
        out_shape=jax.ShapeDtypeStruct((TOKENS, D_MODEL), F32),
        compiler_params=pltpu.CompilerParams(
            dimension_semantics=("arbitrary",), vmem_limit_bytes=VMEM_LIMIT),
        name="mix",
    )(x2d, conv3, conv3, conv3, conv3, conv3, conv3, conv3, bgo, bgo, hm, ada3,
      conv_w, w_conv_out, w_mlstm_out, w_o)


def _ffn_kernel(x1_ref, shift_ref, scale_ref, gate_ref, nw_ref, fnw_ref, wg_ref, wu_ref, wd_ref,
                out_ref, h_scr, acc_scr):
    f = pl.program_id(1)

    @pl.when(f == 0)
    def _():
        _rms_modulate_into(h_scr, x1_ref, nw_ref[...], scale_ref[0], shift_ref[0])
        acc_scr[...] = jnp.zeros_like(acc_scr)

    h = h_scr[...]
    gt = _dot(h, wg_ref[...])
    up = _dot(h, wu_ref[...])
    act = (gt * _sigmoid(gt) * up).astype(BF16)
    acc_scr[...] += _dot(act, wd_ref[...])

    @pl.when(f == NF_FFN - 1)
    def _():
        gate = gate_ref[0]
        fnw = fnw_ref[...]

        def body(rows):
            out_ref[rows, :] = _rms(x1_ref[rows, :] + gate * acc_scr[rows, :], fnw)
        _row_blocks(TM_FFN, body)


def _ffn_call(x1, ada3, norm_w, final_norm_w, w_gate_up, w_down):
    tm = TM_FFN
    tiles_per_seq = SEQ // tm

    def ada_map(k):
        return lambda i, f: ((i // tiles_per_seq) * 6 + k, 0, 0)

    return pl.pallas_call(
        _ffn_kernel,
        grid=(TOKENS // tm, NF_FFN),
        in_specs=[
            pl.BlockSpec((tm, D_MODEL), lambda i, f: (i, 0)),
            pl.BlockSpec((1, 1, D_MODEL), ada_map(3)),
            pl.BlockSpec((1, 1, D_MODEL), ada_map(4)),
            pl.BlockSpec((1, 1, D_MODEL), ada_map(5)),
            pl.BlockSpec((1, D_MODEL), lambda i, f: (0, 0)),
            pl.BlockSpec((1, D_MODEL), lambda i, f: (0, 0)),
            pl.BlockSpec((D_MODEL, TF_FFN), lambda i, f: (0, f)),
            pl.BlockSpec((D_MODEL, TF_FFN), lambda i, f: (0, NF_FFN + f)),
            pl.BlockSpec((TF_FFN, D_MODEL), lambda i, f: (f, 0)),
        ],
        out_specs=pl.BlockSpec((tm, D_MODEL), lambda i, f: (i, 0)),
        out_shape=jax.ShapeDtypeStruct((TOKENS, D_MODEL), F32),
        scratch_shapes=[pltpu.VMEM((tm, D_MODEL), BF16), pltpu.VMEM((tm, D_MODEL), F32)],
        compiler_params=pltpu.CompilerParams(
            dimension_semantics=("arbitrary", "arbitrary"), vmem_limit_bytes=VMEM_LIMIT),
        name="ffn",
    )(x1, ada3, ada3, ada3, norm_w, final_norm_w, w_gate_up, w_gate_up, w_down)


def kernel(x, c, w_ada, b_ada, norm1_w, w_in_mix, conv_w, mlstm_gate_bias, mlstm_norm_w,
           w_conv_out, w_mlstm_out, w_o, norm2_w, w_gate_up, w_down, final_norm_w):
    assert x.shape == (BATCH, SEQ, D_MODEL) and w_ada.shape[0] == 1
    x2d = x.reshape(TOKENS, D_MODEL)

    ada = _ada_call(c, w_ada[0], b_ada[0])
    ada3 = ada.reshape(BATCH * 6, 1, D_MODEL)

    w_in = w_in_mix[0]
    w_a = w_in[:, :OFF_G].astype(BF16)
    w_bg = w_in[:, OFF_BG:].astype(BF16)
    w_gate = jnp.pad(w_in[:, OFF_G:OFF_BG], ((0, 0), (0, LANES - N_GATES))).astype(BF16)
    b_gate = jnp.pad(mlstm_gate_bias[0], (0, LANES - N_GATES)).reshape(1, LANES)

    conv3, q_hm, kt, v_hm, o_hm, bgo, gpre = _inproj_call(
        x2d, ada3, norm1_w[0].reshape(1, D_MODEL), w_a, w_bg, w_gate, b_gate)

    gates = gpre[:, :N_GATES].reshape(BATCH, N_CHUNKS, CHUNK, 4, M_HEADS)
    gates = gates.transpose(0, 4, 3, 1, 2)
    hm = _mlstm_call(gates, q_hm, kt, v_hm, o_hm, mlstm_norm_w[0].reshape(1, M_V))

    x1 = _mix_call(x2d, conv3, bgo, hm, ada3, conv_w[0],
                   w_conv_out[0].astype(BF16), w_mlstm_out[0].astype(BF16), w_o[0].astype(BF16))

    out = _ffn_call(x1, ada3, norm2_w[0].reshape(1, D_MODEL), final_norm_w.reshape(1, D_MODEL),
                    w_gate_up[0].astype(BF16), w_down[0].astype(BF16))
    return out.reshape(BATCH, SEQ, D_MODEL)
```

```python
import functools
import math

import jax
import jax.numpy as jnp
from jax import lax
from jax.experimental import pallas as pl
from jax.experimental.pallas import tpu as pltpu

D_MODEL = 2048
BATCH = 8
SEQ = 4096
TOKENS = BATCH * SEQ
EPS = 1e-6
CONV_WIDTH = D_MODEL // 2
M_HEADS = 8
QK_DIM = D_MODEL // 16
V_DIM = D_MODEL // 8
M_QK = M_HEADS * QK_DIM
M_V = M_HEADS * V_DIM
N_GATES = 4 * M_HEADS
FFN_HIDDEN = int(math.ceil((8 * D_MODEL / 3) / 256) * 256)

OFF_CB, OFF_CC, OFF_CX = 0, CONV_WIDTH, 2 * CONV_WIDTH
OFF_Q = 3 * CONV_WIDTH
OFF_K = OFF_Q + M_QK
OFF_V = OFF_K + M_QK
OFF_O = OFF_V + M_V
OFF_G = OFF_O + M_V
OFF_BG = OFF_G + N_GATES
IN_COLS = OFF_BG + 2 * D_MODEL

MAIN_COLS = IN_COLS - M_QK - N_GATES
MAIN_Q = 3 * CONV_WIDTH
MAIN_V = MAIN_Q + M_QK
MAIN_O = MAIN_V + M_V
MAIN_GC = MAIN_O + M_V
MAIN_GM = MAIN_GC + D_MODEL

LANES = 128
BF16_SUBLANES = 16
VMEM_LIMIT = 56 * 1024 * 1024

CHUNK = 256
N_CHUNKS = SEQ // CHUNK
MLSTM_UNROLL = 4
V_EXT = V_DIM + LANES

TM_IN = 1024
TN_IN = 1024
NJ_A = OFF_G // TN_IN
J_K0 = OFF_K // TN_IN
NJ_K = M_QK // TN_IN
NJ_BG = 2 * D_MODEL // TN_IN
TM_PRE = 512
TM_MIX = 256
TM_FFN = 512
TF_FFN = 512
NF_FFN = FFN_HIDDEN // TF_FFN
TN_ADA = 1024
ROW_BLOCK = 128

F32 = jnp.float32
BF16 = jnp.bfloat16
LOG2E = math.log2(math.e)


def _dot(a, b):
    return jnp.dot(a, b, preferred_element_type=F32)


def _sigmoid(x):
    return 1.0 / (1.0 + jnp.exp(-x))


def _rms(x, norm_w):
    return x * lax.rsqrt(jnp.mean(x * x, axis=-1, keepdims=True) + EPS) * norm_w


def _row_blocks(n_rows, body):
    def step(r, carry):
        body(pl.ds(pl.multiple_of(r * ROW_BLOCK, ROW_BLOCK), ROW_BLOCK))
        return carry
    lax.fori_loop(0, n_rows // ROW_BLOCK, step, 0)


def _rms_modulate_into(dst_ref, x_ref, norm_w, scale, shift):
    gain = norm_w * (1.0 + scale)

    def body(rows):
        dst_ref[rows, :] = (_rms(x_ref[rows, :], gain) + shift).astype(BF16)
    _row_blocks(x_ref.shape[0], body)


def _ada_kernel(c_ref, w_ref, b_ref, o_ref):
    c = c_ref[...]
    c_act = (c * _sigmoid(c)).astype(BF16)
    o_ref[...] = _dot(c_act, w_ref[...].astype(BF16)) + b_ref[...]


def _ada_call(c, w_ada, b_ada):
    n = w_ada.shape[1]
    return pl.pallas_call(
        _ada_kernel,
        grid=(n // TN_ADA,),
        in_specs=[
            pl.BlockSpec((BATCH, D_MODEL), lambda j: (0, 0)),
            pl.BlockSpec((D_MODEL, TN_ADA), lambda j: (0, j)),
            pl.BlockSpec((1, TN_ADA), lambda j: (0, j)),
        ],
        out_specs=pl.BlockSpec((BATCH, TN_ADA), lambda j: (0, j)),
        out_shape=jax.ShapeDtypeStruct((BATCH, n), F32),
        compiler_params=pltpu.CompilerParams(
            dimension_semantics=("arbitrary",), vmem_limit_bytes=VMEM_LIMIT),
        name="ada",
    )(c, w_ada, b_ada.reshape(1, n))


def _prenorm_kernel(x_ref, shift_ref, scale_ref, nw_ref, h_ref):
    _rms_modulate_into(h_ref, x_ref, nw_ref[...], scale_ref[0], shift_ref[0])


def _prenorm_call(x2d, ada3, norm_w):
    tiles_per_seq = SEQ // TM_PRE

    def ada_map(k):
        return lambda i: ((i // tiles_per_seq) * 6 + k, 0, 0)

    return pl.pallas_call(
        _prenorm_kernel,
        grid=(TOKENS // TM_PRE,),
        in_specs=[
            pl.BlockSpec((TM_PRE, D_MODEL), lambda i: (i, 0)),
            pl.BlockSpec((1, 1, D_MODEL), ada_map(0)),
            pl.BlockSpec((1, 1, D_MODEL), ada_map(1)),
            pl.BlockSpec((1, D_MODEL), lambda i: (0, 0)),
        ],
        out_specs=pl.BlockSpec((TM_PRE, D_MODEL), lambda i: (i, 0)),
        out_shape=jax.ShapeDtypeStruct((TOKENS, D_MODEL), BF16),
        compiler_params=pltpu.CompilerParams(
            dimension_semantics=("arbitrary",), vmem_limit_bytes=VMEM_LIMIT),
        name="prenorm",
    )(x2d, ada3, ada3, norm_w)


def _inproj_kernel(h_ref, wa_ref, wbg_ref, wg_ref, bg_ref, main_ref, kt_ref, g_ref):
    j = pl.program_id(1)
    is_k = (j >= J_K0) & (j < J_K0 + NJ_K)

    @pl.when(j == 0)
    def _():
        g_ref[...] = _dot(h_ref[...], wg_ref[...]) + bg_ref[...]

    @pl.when((j < NJ_A) & jnp.logical_not(is_k))
    def _():
        main_ref[...] = _dot(h_ref[...], wa_ref[...]).astype(BF16)

    @pl.when(j >= NJ_A)
    def _():
        main_ref[...] = _dot(h_ref[...], wbg_ref[...]).astype(BF16)

    @pl.when(is_k)
    def _():
        kt = (_dot(h_ref[...], wa_ref[...]) * (QK_DIM ** -0.5)).T
        for p in range(TM_IN // CHUNK):
            kt_ref[p] = kt[:, p * CHUNK:(p + 1) * CHUNK].astype(BF16)


def _inproj_call(h, w_a, w_bg, w_gate, b_gate):
    chunks_per_tile = TM_IN // CHUNK

    def main_col(j):
        return jnp.where(j < J_K0, j, jnp.maximum(j - NJ_K, J_K0 - 1))

    return pl.pallas_call(
        _inproj_kernel,
        grid=(TOKENS // TM_IN, NJ_A + NJ_BG),
        in_specs=[
            pl.BlockSpec((TM_IN, D_MODEL), lambda i, j: (i, 0)),
            pl.BlockSpec((D_MODEL, TN_IN), lambda i, j: (0, jnp.minimum(j, NJ_A - 1))),
            pl.BlockSpec((D_MODEL, TN_IN), lambda i, j: (0, jnp.maximum(j - NJ_A, 0))),
            pl.BlockSpec((D_MODEL, LANES), lambda i, j: (0, 0)),
            pl.BlockSpec((1, LANES), lambda i, j: (0, 0)),
        ],
        out_specs=[
            pl.BlockSpec((TM_IN, TN_IN), lambda i, j: (i, main_col(j))),
            pl.BlockSpec((chunks_per_tile, TN_IN, CHUNK),
                         lambda i, j: (i, jnp.clip(j - J_K0, 0, NJ_K - 1), 0)),
            pl.BlockSpec((TM_IN, LANES), lambda i, j: (i, 0)),
        ],
        out_shape=[
            jax.ShapeDtypeStruct((TOKENS, MAIN_COLS), BF16),
            jax.ShapeDtypeStruct((TOKENS // CHUNK, M_QK, CHUNK), BF16),
            jax.ShapeDtypeStruct((TOKENS, LANES), F32),
        ],
        compiler_params=pltpu.CompilerParams(
            dimension_semantics=("arbitrary", "arbitrary"), vmem_limit_bytes=VMEM_LIMIT),
        name="inproj",
    )(h, w_a, w_bg, w_gate, b_gate)


def _log_sigmoid(x):
    return jnp.minimum(x, 0.0) - jnp.log1p(jnp.exp(-jnp.abs(x)))


def _mlstm_kernel(g_ref, q_ref, kt_ref, v_ref, o_ref, nw_ref, out_ref,
                  rows_scr, h_scr, cf_scr, cb_scr):
    L = CHUNK
    g = g_ref[0, 0]
    row_i = lax.broadcasted_iota(jnp.int32, (L, L), 0)
    col_i = lax.broadcasted_iota(jnp.int32, (L, L), 1)
    lower = col_i <= row_i
    upper = col_i >= row_i

    lf_f = _log_sigmoid(g[1])
    lf_b = _log_sigmoid(g[3])
    cum_f = jnp.dot(lf_f, upper.astype(F32), precision=lax.Precision.HIGHEST,
                    preferred_element_type=F32)
    cum_b = jnp.dot(lf_b, lower.astype(F32), precision=lax.Precision.HIGHEST,
                    preferred_element_type=F32)
    a_f = (g[0] - cum_f) * LOG2E
    a_b = (g[2] - cum_b) * LOG2E
    full = (N_CHUNKS, L)
    rows_scr[0] = lf_f * LOG2E
    rows_scr[1] = a_f
    rows_scr[2] = jnp.broadcast_to(cum_f[:, L - 1:L] * LOG2E, full)
    rows_scr[3] = jnp.broadcast_to(jnp.max(a_f, axis=1, keepdims=True), full)
    rows_scr[4] = lf_b * LOG2E
    rows_scr[5] = a_b
    rows_scr[6] = jnp.broadcast_to(cum_b[:, 0:1] * LOG2E, full)
    rows_scr[7] = jnp.broadcast_to(jnp.max(a_b, axis=1, keepdims=True), full)

    cf_scr[...] = jnp.zeros_like(cf_scr)
    cb_scr[...] = jnp.zeros_like(cb_scr)

    ones_block = jnp.ones((L, LANES), BF16)
    nw = nw_ref[...]

    def chunk(c, tri, base, c_scr, m_row):
        lf_row = rows_scr[base, pl.ds(c, 1), :]
        a_row = rows_scr[base + 1, pl.ds(c, 1), :]
        tot_row = rows_scr[base + 2, pl.ds(c, 1), :]
        amax_row = rows_scr[base + 3, pl.ds(c, 1), :]
        r0 = pl.multiple_of(c * L, L)
        q = q_ref[0, pl.ds(r0, L), :]
        kt = kt_ref[c]
        v_ext = jnp.concatenate([v_ref[0, pl.ds(r0, L), :], ones_block], axis=1)

        cum_col = jnp.sum(jnp.where(tri, lf_row, 0.0), axis=1, keepdims=True)
        a_mat = jnp.where(tri, a_row, -jnp.inf)
        cm_col = jnp.max(a_mat, axis=1, keepdims=True)
        g_mat = jnp.maximum(cm_col, m_row)
        s = (_dot(q, kt) * jnp.exp2(a_mat - g_mat)).astype(BF16)
        g_blk = g_mat[:, :LANES]
        inter = jnp.exp2(m_row[:, :LANES] - g_blk)
        floor = jnp.exp2(-(cum_col + g_blk))
        q_inter = (q.astype(F32) * inter).astype(BF16)
        c_state = c_scr[...]
        num = _dot(s, v_ext) + _dot(q_inter, c_state.astype(BF16))
        den = num[:, V_DIM:]
        r = 1.0 / jnp.maximum(jnp.abs(den), floor)
        h = num[:, :V_DIM] * jnp.tile(r, (1, V_DIM // LANES))

        g_row = jnp.maximum(m_row, amax_row)
        w = jnp.exp2(a_row - g_row)
        decay = jnp.exp2(m_row - g_row)
        ktw = (kt.astype(F32) * w).astype(BF16)
        c_scr[...] = jnp.tile(decay[:, :LANES], (1, V_EXT // LANES)) * c_state + _dot(ktw, v_ext)
        return h, tot_row + g_row

    def finish(c, h_sum):
        r0 = pl.multiple_of(c * L, L)
        hn = h_sum * lax.rsqrt(jnp.mean(h_sum * h_sum, axis=-1, keepdims=True) + EPS) * nw
        out_ref[0, pl.ds(r0, L), :] = _sigmoid(o_ref[0, pl.ds(r0, L), :]) * hn.astype(BF16)

    def step(i, carry, finalize):
        m_f, m_b = carry
        cf = i
        cb = N_CHUNKS - 1 - i
        h_f, m_f = chunk(cf, lower, 0, cf_scr, m_f)
        h_b, m_b = chunk(cb, upper, 4, cb_scr, m_b)
        rf = pl.multiple_of(cf * L, L)
        rb = pl.multiple_of(cb * L, L)
        if finalize:
            finish(cf, h_f + h_scr[pl.ds(rf, L), :])
            finish(cb, h_b + h_scr[pl.ds(rb, L), :])
        else:
            h_scr[pl.ds(rf, L), :] = h_f
            h_scr[pl.ds(rb, L), :] = h_b
        return m_f, m_b

    m0 = jnp.zeros((1, L), F32)
    half = N_CHUNKS // 2
    carry = lax.fori_loop(0, half, functools.partial(step, finalize=False), (m0, m0),
                          unroll=MLSTM_UNROLL)
    lax.fori_loop(half, N_CHUNKS, functools.partial(step, finalize=True), carry,
                  unroll=MLSTM_UNROLL)


def _mlstm_call(gates, main3, kt, norm_w_heads):
    q_blk = MAIN_Q // QK_DIM
    v_blk = MAIN_V // V_DIM
    o_blk = MAIN_O // V_DIM
    return pl.pallas_call(
        _mlstm_kernel,
        grid=(BATCH, M_HEADS),
        in_specs=[
            pl.BlockSpec((1, 1, 4, N_CHUNKS, CHUNK), lambda b, h: (b, h, 0, 0, 0)),
            pl.BlockSpec((1, SEQ, QK_DIM), lambda b, h: (b, 0, q_blk + h)),
            pl.BlockSpec((N_CHUNKS, QK_DIM, CHUNK), lambda b, h: (b, h, 0)),
            pl.BlockSpec((1, SEQ, V_DIM), lambda b, h: (b, 0, v_blk + h)),
            pl.BlockSpec((1, SEQ, V_DIM), lambda b, h: (b, 0, o_blk + h)),
            pl.BlockSpec((1, V_DIM), lambda b, h: (0, h)),
        ],
        out_specs=pl.BlockSpec((1, SEQ, V_DIM), lambda b, h: (b, 0, h)),
        out_shape=jax.ShapeDtypeStruct((BATCH, SEQ, M_V), BF16),
        scratch_shapes=[
            pltpu.VMEM((8, N_CHUNKS, CHUNK), F32),
            pltpu.VMEM((SEQ, V_DIM), F32),
            pltpu.VMEM((QK_DIM, V_EXT), F32),
            pltpu.VMEM((QK_DIM, V_EXT), F32),
        ],
        compiler_params=pltpu.CompilerParams(
            dimension_semantics=("arbitrary", "arbitrary"), vmem_limit_bytes=VMEM_LIMIT),
        name="mlstm",
    )(gates, main3, kt, main3, main3, norm_w_heads)


def _mix_kernel(x_ref, cb_ref, cc_ref, cx_ref, ccp_ref, cxp_ref, ccn_ref, cxn_ref,
                gc_ref, gm_ref, hm_ref, gate1_ref, convw_ref, wconv_ref, wml_ref, wo_ref,
                x1_ref):
    i = pl.program_id(0)
    tm = TM_MIX
    tiles_per_seq = SEQ // tm
    pos = i % tiles_per_seq

    u = cc_ref[...].astype(F32) * cx_ref[...].astype(F32)
    last = BF16_SUBLANES - 1
    u_prev = (ccp_ref[last:last + 1, :].astype(F32) * cxp_ref[last:last + 1, :].astype(F32))
    u_next = ccn_ref[0:1, :].astype(F32) * cxn_ref[0:1, :].astype(F32)
    u_prev = jnp.where(pos == 0, 0.0, u_prev)
    u_next = jnp.where(pos == tiles_per_seq - 1, 0.0, u_next)
    row = lax.broadcasted_iota(jnp.int32, (tm, 1), 0)
    u_m1 = jnp.where(row == 0, u_prev, pltpu.roll(u, 1, 0))
    u_p1 = jnp.where(row == tm - 1, u_next, pltpu.roll(u, tm - 1, 0))
    w = convw_ref[...]
    conv = w[0:1, :] * u_m1 + w[1:2, :] * u + w[2:3, :] * u_p1
    feat = (cb_ref[...].astype(F32) * conv).astype(BF16)
    y_conv = _dot(feat, wconv_ref[...])
    y_mlstm = _dot(hm_ref[...], wml_ref[...])
    merged = (_sigmoid(gc_ref[...].astype(F32)) * y_conv
              + _sigmoid(gm_ref[...].astype(F32)) * y_mlstm).astype(BF16)
    x1_ref[...] = x_ref[...] + gate1_ref[0] * _dot(merged, wo_ref[...])


def _mix_call(x2d, main, hm, ada3, conv_w, w_conv_out, w_mlstm_out, w_o):
    tm = TM_MIX
    tiles_per_seq = SEQ // tm
    halo = BF16_SUBLANES
    n_halo = TOKENS // halo
    per = tm // halo

    def col(k):
        return lambda i: (i, k)

    def prev_map(k):
        return lambda i: (jnp.maximum(i * per - 1, 0), k)

    def next_map(k):
        return lambda i: (jnp.minimum((i + 1) * per, n_halo - 1), k)

    def resident(shape):
        return pl.BlockSpec(shape, lambda i: (0, 0), pipeline_mode=pl.Buffered(1))

    cw = CONV_WIDTH
    return pl.pallas_call(
        _mix_kernel,
        grid=(TOKENS // tm,),
        in_specs=[
            pl.BlockSpec((tm, D_MODEL), lambda i: (i, 0)),
            pl.BlockSpec((tm, cw), col(0)),
            pl.BlockSpec((tm, cw), col(1)),
            pl.BlockSpec((tm, cw), col(2)),
            pl.BlockSpec((halo, cw), prev_map(1)),
            pl.BlockSpec((halo, cw), prev_map(2)),
            pl.BlockSpec((halo, cw), next_map(1)),
            pl.BlockSpec((halo, cw), next_map(2)),
            pl.BlockSpec((tm, D_MODEL), col(MAIN_GC // D_MODEL)),
            pl.BlockSpec((tm, D_MODEL), col(MAIN_GM // D_MODEL)),
            pl.BlockSpec((tm, M_V), lambda i: (i, 0)),
            pl.BlockSpec((1, 1, D_MODEL), lambda i: ((i // tiles_per_seq) * 6 + 2, 0, 0)),
            resident((3, cw)),
            resident((cw, D_MODEL)),
            resident((M_V, D_MODEL)),
            resident((D_MODEL, D_MODEL)),
        ],
        out_specs=pl.BlockSpec((tm, D_MODEL), lambda i: (i, 0)),
        out_shape=jax.ShapeDtypeStruct((TOKENS, D_MODEL), F32),
        compiler_params=pltpu.CompilerParams(
            dimension_semantics=("arbitrary",), vmem_limit_bytes=VMEM_LIMIT),
        name="mix",
    )(x2d, main, main, main, main, main, main, main, main, main, hm, ada3,
      conv_w, w_conv_out, w_mlstm_out, w_o)


def _ffn_kernel(x1_ref, shift_ref, scale_ref, gate_ref, nw_ref, fnw_ref, wg_ref, wu_ref, wd_ref,
                out_ref, h_scr, acc_scr):
    f = pl.program_id(1)

    @pl.when(f == 0)
    def _():
        _rms_modulate_into(h_scr, x1_ref, nw_ref[...], scale_ref[0], shift_ref[0])
        acc_scr[...] = jnp.zeros_like(acc_scr)

    h = h_scr[...]
    gt = _dot(h, wg_ref[...])
    up = _dot(h, wu_ref[...])
    act = (gt * _sigmoid(gt) * up).astype(BF16)
    acc_scr[...] += _dot(act, wd_ref[...])

    @pl.when(f == NF_FFN - 1)
    def _():
        gate = gate_ref[0]
        fnw = fnw_ref[...]

        def body(rows):
            out_ref[rows, :] = _rms(x1_ref[rows, :] + gate * acc_scr[rows, :], fnw)
        _row_blocks(TM_FFN, body)


def _ffn_call(x1, ada3, norm_w, final_norm_w, w_gate_up, w_down):
    tm = TM_FFN
    tiles_per_seq = SEQ // tm

    def ada_map(k):
        return lambda i, f: ((i // tiles_per_seq) * 6 + k, 0, 0)

    return pl.pallas_call(
        _ffn_kernel,
        grid=(TOKENS // tm, NF_FFN),
        in_specs=[
            pl.BlockSpec((tm, D_MODEL), lambda i, f: (i, 0)),
            pl.BlockSpec((1, 1, D_MODEL), ada_map(3)),
            pl.BlockSpec((1, 1, D_MODEL), ada_map(4)),
            pl.BlockSpec((1, 1, D_MODEL), ada_map(5)),
            pl.BlockSpec((1, D_MODEL), lambda i, f: (0, 0)),
            pl.BlockSpec((1, D_MODEL), lambda i, f: (0, 0)),
            pl.BlockSpec((D_MODEL, TF_FFN), lambda i, f: (0, f)),
            pl.BlockSpec((D_MODEL, TF_FFN), lambda i, f: (0, NF_FFN + f)),
            pl.BlockSpec((TF_FFN, D_MODEL), lambda i, f: (f, 0)),
        ],
        out_specs=pl.BlockSpec((tm, D_MODEL), lambda i, f: (i, 0)),
        out_shape=jax.ShapeDtypeStruct((TOKENS, D_MODEL), F32),
        scratch_shapes=[pltpu.VMEM((tm, D_MODEL), BF16), pltpu.VMEM((tm, D_MODEL), F32)],
        compiler_params=pltpu.CompilerParams(
            dimension_semantics=("arbitrary", "arbitrary"), vmem_limit_bytes=VMEM_LIMIT),
        name="ffn",
    )(x1, ada3, ada3, ada3, norm_w, final_norm_w, w_gate_up, w_gate_up, w_down)


def kernel(x, c, w_ada, b_ada, norm1_w, w_in_mix, conv_w, mlstm_gate_bias, mlstm_norm_w,
           w_conv_out, w_mlstm_out, w_o, norm2_w, w_gate_up, w_down, final_norm_w):
    assert x.shape == (BATCH, SEQ, D_MODEL) and w_ada.shape[0] == 1
    x2d = x.reshape(TOKENS, D_MODEL)

    ada = _ada_call(c, w_ada[0], b_ada[0])
    ada3 = ada.reshape(BATCH * 6, 1, D_MODEL)

    w_in = w_in_mix[0]
    w_a = w_in.astype(BF16)
    w_bg = w_in[:, OFF_BG:].astype(BF16)
    w_gate = jnp.pad(w_in[:, OFF_G:OFF_BG], ((0, 0), (0, LANES - N_GATES))).astype(BF16)
    b_gate = jnp.pad(mlstm_gate_bias[0], (0, LANES - N_GATES)).reshape(1, LANES)

    h = _prenorm_call(x2d, ada3, norm1_w[0].reshape(1, D_MODEL))
    main, kt, gpre = _inproj_call(h, w_a, w_bg, w_gate, b_gate)

    gates = gpre[:, :N_GATES].reshape(BATCH, N_CHUNKS, CHUNK, 4, M_HEADS)
    gates = gates.transpose(0, 4, 3, 1, 2)
    hm = _mlstm_call(gates, main.reshape(BATCH, SEQ, MAIN_COLS), kt,
                     mlstm_norm_w[0].reshape(1, M_V))

    x1 = _mix_call(x2d, main, hm.reshape(TOKENS, M_V), ada3, conv_w[0],
                   w_conv_out[0].astype(BF16), w_mlstm_out[0].astype(BF16), w_o[0].astype(BF16))

    out = _ffn_call(x1, ada3, norm2_w[0].reshape(1, D_MODEL), final_norm_w.reshape(1, D_MODEL),
                    w_gate_up[0].astype(BF16), w_down[0].astype(BF16))
    return out.reshape(BATCH, SEQ, D_MODEL)
```

```python
import functools
import math

import jax
import jax.numpy as jnp
from jax import lax
from jax.experimental import pallas as pl
from jax.experimental.pallas import tpu as pltpu

D_MODEL = 2048
BATCH = 8
SEQ = 4096
TOKENS = BATCH * SEQ
EPS = 1e-6
CONV_WIDTH = D_MODEL // 2
M_HEADS = 8
QK_DIM = D_MODEL // 16
V_DIM = D_MODEL // 8
M_QK = M_HEADS * QK_DIM
M_V = M_HEADS * V_DIM
N_GATES = 4 * M_HEADS
FFN_HIDDEN = int(math.ceil((8 * D_MODEL / 3) / 256) * 256)

OFF_CB, OFF_CC, OFF_CX = 0, CONV_WIDTH, 2 * CONV_WIDTH
OFF_Q = 3 * CONV_WIDTH
OFF_K = OFF_Q + M_QK
OFF_V = OFF_K + M_QK
OFF_O = OFF_V + M_V
OFF_G = OFF_O + M_V
OFF_BG = OFF_G + N_GATES
IN_COLS = OFF_BG + 2 * D_MODEL

MAIN_COLS = IN_COLS - M_QK - N_GATES
MAIN_Q = 3 * CONV_WIDTH
MAIN_V = MAIN_Q + M_QK
MAIN_O = MAIN_V + M_V
MAIN_GC = MAIN_O + M_V
MAIN_GM = MAIN_GC + D_MODEL

LANES = 128
BF16_SUBLANES = 16
VMEM_LIMIT = 56 * 1024 * 1024

CHUNK = 256
N_CHUNKS = SEQ // CHUNK
MLSTM_UNROLL = 4
V_EXT = V_DIM + LANES

TM_IN = 1024
TN_IN = 1024
NJ_A = OFF_G // TN_IN
J_K0 = OFF_K // TN_IN
NJ_K = M_QK // TN_IN
NJ_BG = 2 * D_MODEL // TN_IN
assert NJ_K == 1 and OFF_G % TN_IN == 0
PRE_BLOCKS = 8
TM_MIX = 256
TM_FFN = 512
TF_FFN = 512
NF_FFN = FFN_HIDDEN // TF_FFN
TN_ADA = 1024
ROW_BLOCK = 128

F32 = jnp.float32
BF16 = jnp.bfloat16
LOG2E = math.log2(math.e)


def _dot(a, b):
    return jnp.dot(a, b, preferred_element_type=F32)


def _sigmoid(x):
    return 1.0 / (1.0 + jnp.exp(-x))


def _rms(x, norm_w):
    return x * lax.rsqrt(jnp.mean(x * x, axis=-1, keepdims=True) + EPS) * norm_w


def _row_blocks(n_rows, body):
    def step(r, carry):
        body(pl.ds(pl.multiple_of(r * ROW_BLOCK, ROW_BLOCK), ROW_BLOCK))
        return carry
    lax.fori_loop(0, n_rows // ROW_BLOCK, step, 0)


def _rms_modulate_into(dst_ref, x_ref, norm_w, scale, shift):
    gain = norm_w * (1.0 + scale)

    def body(rows):
        dst_ref[rows, :] = (_rms(x_ref[rows, :], gain) + shift).astype(BF16)
    _row_blocks(x_ref.shape[0], body)


def _ada_kernel(c_ref, w_ref, b_ref, o_ref):
    c = c_ref[...]
    c_act = (c * _sigmoid(c)).astype(BF16)
    o_ref[...] = _dot(c_act, w_ref[...].astype(BF16)) + b_ref[...]


def _ada_call(c, w_ada, b_ada):
    n = w_ada.shape[1]
    return pl.pallas_call(
        _ada_kernel,
        grid=(n // TN_ADA,),
        in_specs=[
            pl.BlockSpec((BATCH, D_MODEL), lambda j: (0, 0)),
            pl.BlockSpec((D_MODEL, TN_ADA), lambda j: (0, j)),
            pl.BlockSpec((1, TN_ADA), lambda j: (0, j)),
        ],
        out_specs=pl.BlockSpec((BATCH, TN_ADA), lambda j: (0, j)),
        out_shape=jax.ShapeDtypeStruct((BATCH, n), F32),
        compiler_params=pltpu.CompilerParams(
            dimension_semantics=("arbitrary",), vmem_limit_bytes=VMEM_LIMIT),
        name="ada",
    )(c, w_ada, b_ada.reshape(1, n))


def _next_tile_rows(step, n_rows):
    rows = n_rows // PRE_BLOCKS
    blk = jnp.minimum(step, PRE_BLOCKS - 1)
    return pl.ds(pl.multiple_of(blk * rows, rows), rows)


def _inproj_kernel(x_ref, shift_ref, scale_ref, nw_ref, w_ref, wg_ref, bg_ref,
                   main_ref, kt_ref, g_ref, h_scr):
    r = pl.program_id(0)
    j = pl.program_id(1)
    cur = (r + 1) % 2
    nxt = r % 2
    is_k = (j >= J_K0) & (j < J_K0 + NJ_K)
    gain = nw_ref[...] * (1.0 + scale_ref[0])
    shift = shift_ref[0]

    def norm_next_rows():
        rows = _next_tile_rows(j, TM_IN)
        h_scr[nxt, rows, :] = (_rms(x_ref[rows, :], gain) + shift).astype(BF16)

    @pl.when(r == 0)
    def _():
        norm_next_rows()

    @pl.when((r > 0) & (j == 0))
    def _():
        g_ref[...] = _dot(h_scr[cur], wg_ref[...]) + bg_ref[...]

    @pl.when((r > 0) & jnp.logical_not(is_k))
    def _():
        main_ref[...] = _dot(h_scr[cur], w_ref[...]).astype(BF16)
        norm_next_rows()

    @pl.when((r > 0) & is_k)
    def _():
        kt = (_dot(h_scr[cur], w_ref[...]) * (QK_DIM ** -0.5)).T
        for p in range(TM_IN // CHUNK):
            kt_ref[p] = kt[:, p * CHUNK:(p + 1) * CHUNK].astype(BF16)
        norm_next_rows()


def _inproj_call(x2d, ada3, norm_w, w_all, w_gate, b_gate):
    n_tiles = TOKENS // TM_IN
    tiles_per_seq = SEQ // TM_IN
    chunks_per_tile = TM_IN // CHUNK

    def next_tile(r):
        return jnp.minimum(r, n_tiles - 1)

    def this_tile(r):
        return jnp.maximum(r - 1, 0)

    def ada_map(k):
        return lambda r, j: ((next_tile(r) // tiles_per_seq) * 6 + k, 0, 0)

    def main_col(r, j):
        col = jnp.where(j < J_K0, j, jnp.maximum(j - NJ_K, J_K0 - 1))
        return jnp.where(r > 0, col, 0)

    return pl.pallas_call(
        _inproj_kernel,
        grid=(n_tiles + 1, NJ_A + NJ_BG),
        in_specs=[
            pl.BlockSpec((TM_IN, D_MODEL), lambda r, j: (next_tile(r), 0)),
            pl.BlockSpec((1, 1, D_MODEL), ada_map(0)),
            pl.BlockSpec((1, 1, D_MODEL), ada_map(1)),
            pl.BlockSpec((1, D_MODEL), lambda r, j: (0, 0)),
            pl.BlockSpec((D_MODEL, TN_IN), lambda r, j: (0, j)),
            pl.BlockSpec((D_MODEL, LANES), lambda r, j: (0, 0)),
            pl.BlockSpec((1, LANES), lambda r, j: (0, 0)),
        ],
        out_specs=[
            pl.BlockSpec((TM_IN, TN_IN), lambda r, j: (this_tile(r), main_col(r, j))),
            pl.BlockSpec((chunks_per_tile, M_QK, CHUNK), lambda r, j: (this_tile(r), 0, 0)),
            pl.BlockSpec((TM_IN, LANES), lambda r, j: (this_tile(r), 0)),
        ],
        out_shape=[
            jax.ShapeDtypeStruct((TOKENS, MAIN_COLS), BF16),
            jax.ShapeDtypeStruct((TOKENS // CHUNK, M_QK, CHUNK), BF16),
            jax.ShapeDtypeStruct((TOKENS, LANES), F32),
        ],
        scratch_shapes=[pltpu.VMEM((2, TM_IN, D_MODEL), BF16)],
        compiler_params=pltpu.CompilerParams(
            dimension_semantics=("arbitrary", "arbitrary"), vmem_limit_bytes=VMEM_LIMIT),
        name="inproj",
    )(x2d, ada3, ada3, norm_w, w_all, w_gate, b_gate)


def _log_sigmoid(x):
    return jnp.minimum(x, 0.0) - jnp.log1p(jnp.exp(-jnp.abs(x)))


def _mlstm_kernel(g_ref, q_ref, kt_ref, v_ref, o_ref, nw_ref, out_ref,
                  rows_scr, h_scr, cf_scr, cb_scr):
    L = CHUNK
    g = g_ref[0, 0]
    row_i = lax.broadcasted_iota(jnp.int32, (L, L), 0)
    col_i = lax.broadcasted_iota(jnp.int32, (L, L), 1)
    lower = col_i <= row_i
    upper = col_i >= row_i

    lf_f = _log_sigmoid(g[1])
    lf_b = _log_sigmoid(g[3])
    cum_f = jnp.dot(lf_f, upper.astype(F32), precision=lax.Precision.HIGHEST,
                    preferred_element_type=F32)
    cum_b = jnp.dot(lf_b, lower.astype(F32), precision=lax.Precision.HIGHEST,
                    preferred_element_type=F32)
    a_f = (g[0] - cum_f) * LOG2E
    a_b = (g[2] - cum_b) * LOG2E
    full = (N_CHUNKS, L)
    rows_scr[0] = lf_f * LOG2E
    rows_scr[1] = a_f
    rows_scr[2] = jnp.broadcast_to(cum_f[:, L - 1:L] * LOG2E, full)
    rows_scr[3] = jnp.broadcast_to(jnp.max(a_f, axis=1, keepdims=True), full)
    rows_scr[4] = lf_b * LOG2E
    rows_scr[5] = a_b
    rows_scr[6] = jnp.broadcast_to(cum_b[:, 0:1] * LOG2E, full)
    rows_scr[7] = jnp.broadcast_to(jnp.max(a_b, axis=1, keepdims=True), full)

    cf_scr[...] = jnp.zeros_like(cf_scr)
    cb_scr[...] = jnp.zeros_like(cb_scr)

    ones_block = jnp.ones((L, LANES), BF16)
    nw = nw_ref[...]

    def chunk(c, tri, base, c_scr, m_row):
        lf_row = rows_scr[base, pl.ds(c, 1), :]
        a_row = rows_scr[base + 1, pl.ds(c, 1), :]
        tot_row = rows_scr[base + 2, pl.ds(c, 1), :]
        amax_row = rows_scr[base + 3, pl.ds(c, 1), :]
        r0 = pl.multiple_of(c * L, L)
        q = q_ref[0, pl.ds(r0, L), :]
        kt = kt_ref[c]
        v_ext = jnp.concatenate([v_ref[0, pl.ds(r0, L), :], ones_block], axis=1)

        cum_col = jnp.sum(jnp.where(tri, lf_row, 0.0), axis=1, keepdims=True)
        a_mat = jnp.where(tri, a_row, -jnp.inf)
        cm_col = jnp.max(a_mat, axis=1, keepdims=True)
        g_mat = jnp.maximum(cm_col, m_row)
        s = (_dot(q, kt) * jnp.exp2(a_mat - g_mat)).astype(BF16)
        g_blk = g_mat[:, :LANES]
        inter = jnp.exp2(m_row[:, :LANES] - g_blk)
        floor = jnp.exp2(-(cum_col + g_blk))
        q_inter = (q.astype(F32) * inter).astype(BF16)
        c_state = c_scr[...]
        num = _dot(s, v_ext) + _dot(q_inter, c_state.astype(BF16))
        den = num[:, V_DIM:]
        r = 1.0 / jnp.maximum(jnp.abs(den), floor)
        h = num[:, :V_DIM] * jnp.tile(r, (1, V_DIM // LANES))

        g_row = jnp.maximum(m_row, amax_row)
        w = jnp.exp2(a_row - g_row)
        decay = jnp.exp2(m_row - g_row)
        ktw = (kt.astype(F32) * w).astype(BF16)
        c_scr[...] = jnp.tile(decay[:, :LANES], (1, V_EXT // LANES)) * c_state + _dot(ktw, v_ext)
        return h, tot_row + g_row

    def finish(c, h_sum):
        r0 = pl.multiple_of(c * L, L)
        hn = h_sum * lax.rsqrt(jnp.mean(h_sum * h_sum, axis=-1, keepdims=True) + EPS) * nw
        out_ref[0, pl.ds(r0, L), :] = _sigmoid(o_ref[0, pl.ds(r0, L), :]) * hn.astype(BF16)

    def step(i, carry, finalize):
        m_f, m_b = carry
        cf = i
        cb = N_CHUNKS - 1 - i
        h_f, m_f = chunk(cf, lower, 0, cf_scr, m_f)
        h_b, m_b = chunk(cb, upper, 4, cb_scr, m_b)
        rf = pl.multiple_of(cf * L, L)
        rb = pl.multiple_of(cb * L, L)
        if finalize:
            finish(cf, h_f + h_scr[pl.ds(rf, L), :])
            finish(cb, h_b + h_scr[pl.ds(rb, L), :])
        else:
            h_scr[pl.ds(rf, L), :] = h_f
            h_scr[pl.ds(rb, L), :] = h_b
        return m_f, m_b

    m0 = jnp.zeros((1, L), F32)
    half = N_CHUNKS // 2
    carry = lax.fori_loop(0, half, functools.partial(step, finalize=False), (m0, m0),
                          unroll=MLSTM_UNROLL)
    lax.fori_loop(half, N_CHUNKS, functools.partial(step, finalize=True), carry,
                  unroll=MLSTM_UNROLL)


def _mlstm_call(gates, main3, kt, norm_w_heads):
    q_blk = MAIN_Q // QK_DIM
    v_blk = MAIN_V // V_DIM
    o_blk = MAIN_O // V_DIM
    return pl.pallas_call(
        _mlstm_kernel,
        grid=(BATCH, M_HEADS),
        in_specs=[
            pl.BlockSpec((1, 1, 4, N_CHUNKS, CHUNK), lambda b, h: (b, h, 0, 0, 0)),
            pl.BlockSpec((1, SEQ, QK_DIM), lambda b, h: (b, 0, q_blk + h)),
            pl.BlockSpec((N_CHUNKS, QK_DIM, CHUNK), lambda b, h: (b, h, 0)),
            pl.BlockSpec((1, SEQ, V_DIM), lambda b, h: (b, 0, v_blk + h)),
            pl.BlockSpec((1, SEQ, V_DIM), lambda b, h: (b, 0, o_blk + h)),
            pl.BlockSpec((1, V_DIM), lambda b, h: (0, h)),
        ],
        out_specs=pl.BlockSpec((1, SEQ, V_DIM), lambda b, h: (b, 0, h)),
        out_shape=jax.ShapeDtypeStruct((BATCH, SEQ, M_V), BF16),
        scratch_shapes=[
            pltpu.VMEM((8, N_CHUNKS, CHUNK), F32),
            pltpu.VMEM((SEQ, V_DIM), F32),
            pltpu.VMEM((QK_DIM, V_EXT), F32),
            pltpu.VMEM((QK_DIM, V_EXT), F32),
        ],
        compiler_params=pltpu.CompilerParams(
            dimension_semantics=("arbitrary", "arbitrary"), vmem_limit_bytes=VMEM_LIMIT),
        name="mlstm",
    )(gates, main3, kt, main3, main3, norm_w_heads)


def _mix_kernel(x_ref, cb_ref, cc_ref, cx_ref, ccp_ref, cxp_ref, ccn_ref, cxn_ref,
                gc_ref, gm_ref, hm_ref, gate1_ref, convw_ref, wconv_ref, wml_ref, wo_ref,
                x1_ref):
    i = pl.program_id(0)
    tm = TM_MIX
    tiles_per_seq = SEQ // tm
    pos = i % tiles_per_seq

    u = cc_ref[...].astype(F32) * cx_ref[...].astype(F32)
    last = BF16_SUBLANES - 1
    u_prev = (ccp_ref[last:last + 1, :].astype(F32) * cxp_ref[last:last + 1, :].astype(F32))
    u_next = ccn_ref[0:1, :].astype(F32) * cxn_ref[0:1, :].astype(F32)
    u_prev = jnp.where(pos == 0, 0.0, u_prev)
    u_next = jnp.where(pos == tiles_per_seq - 1, 0.0, u_next)
    row = lax.broadcasted_iota(jnp.int32, (tm, 1), 0)
    u_m1 = jnp.where(row == 0, u_prev, pltpu.roll(u, 1, 0))
    u_p1 = jnp.where(row == tm - 1, u_next, pltpu.roll(u, tm - 1, 0))
    w = convw_ref[...]
    conv = w[0:1, :] * u_m1 + w[1:2, :] * u + w[2:3, :] * u_p1
    feat = (cb_ref[...].astype(F32) * conv).astype(BF16)
    y_conv = _dot(feat, wconv_ref[...])
    y_mlstm = _dot(hm_ref[...], wml_ref[...])
    merged = (_sigmoid(gc_ref[...].astype(F32)) * y_conv
              + _sigmoid(gm_ref[...].astype(F32)) * y_mlstm).astype(BF16)
    x1_ref[...] = x_ref[...] + gate1_ref[0] * _dot(merged, wo_ref[...])


def _mix_call(x2d, main, hm, ada3, conv_w, w_conv_out, w_mlstm_out, w_o):
    tm = TM_MIX
    tiles_per_seq = SEQ // tm
    halo = BF16_SUBLANES
    n_halo = TOKENS // halo
    per = tm // halo

    def col(k):
        return lambda i: (i, k)

    def prev_map(k):
        return lambda i: (jnp.maximum(i * per - 1, 0), k)

    def next_map(k):
        return lambda i: (jnp.minimum((i + 1) * per, n_halo - 1), k)

    def resident(shape):
        return pl.BlockSpec(shape, lambda i: (0, 0), pipeline_mode=pl.Buffered(1))

    cw = CONV_WIDTH
    return pl.pallas_call(
        _mix_kernel,
        grid=(TOKENS // tm,),
        in_specs=[
            pl.BlockSpec((tm, D_MODEL), lambda i: (i, 0)),
            pl.BlockSpec((tm, cw), col(0)),
            pl.BlockSpec((tm, cw), col(1)),
            pl.BlockSpec((tm, cw), col(2)),
            pl.BlockSpec((halo, cw), prev_map(1)),
            pl.BlockSpec((halo, cw), prev_map(2)),
            pl.BlockSpec((halo, cw), next_map(1)),
            pl.BlockSpec((halo, cw), next_map(2)),
            pl.BlockSpec((tm, D_MODEL), col(MAIN_GC // D_MODEL)),
            pl.BlockSpec((tm, D_MODEL), col(MAIN_GM // D_MODEL)),
            pl.BlockSpec((tm, M_V), lambda i: (i, 0)),
            pl.BlockSpec((1, 1, D_MODEL), lambda i: ((i // tiles_per_seq) * 6 + 2, 0, 0)),
            resident((3, cw)),
            resident((cw, D_MODEL)),
            resident((M_V, D_MODEL)),
            resident((D_MODEL, D_MODEL)),
        ],
        out_specs=pl.BlockSpec((tm, D_MODEL), lambda i: (i, 0)),
        out_shape=jax.ShapeDtypeStruct((TOKENS, D_MODEL), F32),
        compiler_params=pltpu.CompilerParams(
            dimension_semantics=("arbitrary",), vmem_limit_bytes=VMEM_LIMIT),
        name="mix",
    )(x2d, main, main, main, main, main, main, main, main, main, hm, ada3,
      conv_w, w_conv_out, w_mlstm_out, w_o)


def _ffn_kernel(x1n_ref, x1_ref, shift_ref, scale_ref, gate_ref, nw_ref, fnw_ref,
                wg_ref, wu_ref, wd_ref, out_ref, h_scr, acc_scr):
    r = pl.program_id(0)
    f = pl.program_id(1)
    cur = (r + 1) % 2
    nxt = r % 2
    gain = nw_ref[...] * (1.0 + scale_ref[0])
    shift = shift_ref[0]

    def norm_next_rows():
        rows = _next_tile_rows(f, TM_FFN)
        h_scr[nxt, rows, :] = (_rms(x1n_ref[rows, :], gain) + shift).astype(BF16)

    @pl.when(r == 0)
    def _():
        norm_next_rows()

    @pl.when((r > 0) & (f == 0))
    def _():
        acc_scr[...] = jnp.zeros_like(acc_scr)

    @pl.when(r > 0)
    def _():
        h = h_scr[cur]
        gt = _dot(h, wg_ref[...])
        up = _dot(h, wu_ref[...])
        act = (gt * _sigmoid(gt) * up).astype(BF16)
        acc_scr[...] += _dot(act, wd_ref[...])
        norm_next_rows()

    @pl.when((r > 0) & (f == NF_FFN - 1))
    def _():
        gate = gate_ref[0]
        fnw = fnw_ref[...]

        def body(rows):
            out_ref[rows, :] = _rms(x1_ref[rows, :] + gate * acc_scr[rows, :], fnw)
        _row_blocks(TM_FFN, body)


def _ffn_call(x1, ada3, norm_w, final_norm_w, w_gate_up, w_down):
    tm = TM_FFN
    n_tiles = TOKENS // tm
    tiles_per_seq = SEQ // tm

    def next_tile(r):
        return jnp.minimum(r, n_tiles - 1)

    def this_tile(r):
        return jnp.maximum(r - 1, 0)

    def ada_map(k, tile):
        return lambda r, f: ((tile(r) // tiles_per_seq) * 6 + k, 0, 0)

    return pl.pallas_call(
        _ffn_kernel,
        grid=(n_tiles + 1, NF_FFN),
        in_specs=[
            pl.BlockSpec((tm, D_MODEL), lambda r, f: (next_tile(r), 0)),
            pl.BlockSpec((tm, D_MODEL), lambda r, f: (this_tile(r), 0)),
            pl.BlockSpec((1, 1, D_MODEL), ada_map(3, next_tile)),
            pl.BlockSpec((1, 1, D_MODEL), ada_map(4, next_tile)),
            pl.BlockSpec((1, 1, D_MODEL), ada_map(5, this_tile)),
            pl.BlockSpec((1, D_MODEL), lambda r, f: (0, 0)),
            pl.BlockSpec((1, D_MODEL), lambda r, f: (0, 0)),
            pl.BlockSpec((D_MODEL, TF_FFN), lambda r, f: (0, f)),
            pl.BlockSpec((D_MODEL, TF_FFN), lambda r, f: (0, NF_FFN + f)),
            pl.BlockSpec((TF_FFN, D_MODEL), lambda r, f: (f, 0)),
        ],
        out_specs=pl.BlockSpec((tm, D_MODEL), lambda r, f: (this_tile(r), 0)),
        out_shape=jax.ShapeDtypeStruct((TOKENS, D_MODEL), F32),
        scratch_shapes=[pltpu.VMEM((2, tm, D_MODEL), BF16), pltpu.VMEM((tm, D_MODEL), F32)],
        compiler_params=pltpu.CompilerParams(
            dimension_semantics=("arbitrary", "arbitrary"), vmem_limit_bytes=VMEM_LIMIT),
        name="ffn",
    )(x1, x1, ada3, ada3, ada3, norm_w, final_norm_w, w_gate_up, w_gate_up, w_down)


def kernel(x, c, w_ada, b_ada, norm1_w, w_in_mix, conv_w, mlstm_gate_bias, mlstm_norm_w,
           w_conv_out, w_mlstm_out, w_o, norm2_w, w_gate_up, w_down, final_norm_w):
    assert x.shape == (BATCH, SEQ, D_MODEL) and w_ada.shape[0] == 1
    x2d = x.reshape(TOKENS, D_MODEL)

    ada = _ada_call(c, w_ada[0], b_ada[0])
    ada3 = ada.reshape(BATCH * 6, 1, D_MODEL)

    w_in = w_in_mix[0]
    w_all = lax.dynamic_update_slice(w_in.astype(BF16), w_in[:, OFF_BG:].astype(BF16), (0, OFF_G))
    w_gate = jnp.pad(w_in[:, OFF_G:OFF_BG], ((0, 0), (0, LANES - N_GATES))).astype(BF16)
    b_gate = jnp.pad(mlstm_gate_bias[0], (0, LANES - N_GATES)).reshape(1, LANES)

    main, kt, gpre = _inproj_call(x2d, ada3, norm1_w[0].reshape(1, D_MODEL),
                                  w_all, w_gate, b_gate)

    gates = gpre[:, :N_GATES].reshape(BATCH, N_CHUNKS, CHUNK, 4, M_HEADS)
    gates = gates.transpose(0, 4, 3, 1, 2)
    hm = _mlstm_call(gates, main.reshape(BATCH, SEQ, MAIN_COLS), kt,
                     mlstm_norm_w[0].reshape(1, M_V))

    x1 = _mix_call(x2d, main, hm.reshape(TOKENS, M_V), ada3, conv_w[0],
                   w_conv_out[0].astype(BF16), w_mlstm_out[0].astype(BF16), w_o[0].astype(BF16))

    out = _ffn_call(x1, ada3, norm2_w[0].reshape(1, D_MODEL), final_norm_w.reshape(1, D_MODEL),
                    w_gate_up[0].astype(BF16), w_down[0].astype(BF16))
    return out.reshape(BATCH, SEQ, D_MODEL)
```

```python
import functools
import math

import jax
import jax.numpy as jnp
from jax import lax
from jax.experimental import pallas as pl
from jax.experimental.pallas import tpu as pltpu

D_MODEL = 2048
BATCH = 8
SEQ = 4096
TOKENS = BATCH * SEQ
EPS = 1e-6
CONV_WIDTH = D_MODEL // 2
M_HEADS = 8
QK_DIM = D_MODEL // 16
V_DIM = D_MODEL // 8
M_QK = M_HEADS * QK_DIM
M_V = M_HEADS * V_DIM
N_GATES = 4 * M_HEADS
FFN_HIDDEN = int(math.ceil((8 * D_MODEL / 3) / 256) * 256)

OFF_CB, OFF_CC, OFF_CX = 0, CONV_WIDTH, 2 * CONV_WIDTH
OFF_Q = 3 * CONV_WIDTH
OFF_K = OFF_Q + M_QK
OFF_V = OFF_K + M_QK
OFF_O = OFF_V + M_V
OFF_G = OFF_O + M_V
OFF_BG = OFF_G + N_GATES
IN_COLS = OFF_BG + 2 * D_MODEL

MAIN_COLS = IN_COLS - M_QK - N_GATES
MAIN_Q = 3 * CONV_WIDTH
MAIN_V = MAIN_Q + M_QK
MAIN_O = MAIN_V + M_V
MAIN_GC = MAIN_O + M_V
MAIN_GM = MAIN_GC + D_MODEL

LANES = 128
BF16_SUBLANES = 16
VMEM_LIMIT = 56 * 1024 * 1024

CHUNK = 256
N_CHUNKS = SEQ // CHUNK
MLSTM_UNROLL = 4
V_EXT = V_DIM + LANES

TM_IN = 1024
TN_IN = 1024
NJ_A = OFF_G // TN_IN
J_K0 = OFF_K // TN_IN
NJ_K = M_QK // TN_IN
NJ_BG = 2 * D_MODEL // TN_IN
assert NJ_K == 1 and OFF_G % TN_IN == 0
PRE_BLOCKS = 8
TM_MIX = 256
TM_UP = 1024
TM_DOWN = 256
TF_FFN = 512
NF_FFN = FFN_HIDDEN // TF_FFN
TN_ADA = 1024
ROW_BLOCK = 128

F32 = jnp.float32
BF16 = jnp.bfloat16
LOG2E = math.log2(math.e)


def _dot(a, b):
    return jnp.dot(a, b, preferred_element_type=F32)


def _sigmoid(x):
    return 1.0 / (1.0 + jnp.exp(-x))


def _rms(x, norm_w):
    return x * lax.rsqrt(jnp.mean(x * x, axis=-1, keepdims=True) + EPS) * norm_w


def _row_blocks(n_rows, body):
    def step(r, carry):
        body(pl.ds(pl.multiple_of(r * ROW_BLOCK, ROW_BLOCK), ROW_BLOCK))
        return carry
    lax.fori_loop(0, n_rows // ROW_BLOCK, step, 0)


def _rms_modulate_into(dst_ref, x_ref, norm_w, scale, shift):
    gain = norm_w * (1.0 + scale)

    def body(rows):
        dst_ref[rows, :] = (_rms(x_ref[rows, :], gain) + shift).astype(BF16)
    _row_blocks(x_ref.shape[0], body)


def _ada_kernel(c_ref, w_ref, b_ref, o_ref):
    c = c_ref[...]
    c_act = (c * _sigmoid(c)).astype(BF16)
    o_ref[...] = _dot(c_act, w_ref[...].astype(BF16)) + b_ref[...]


def _ada_call(c, w_ada, b_ada):
    n = w_ada.shape[1]
    return pl.pallas_call(
        _ada_kernel,
        grid=(n // TN_ADA,),
        in_specs=[
            pl.BlockSpec((BATCH, D_MODEL), lambda j: (0, 0)),
            pl.BlockSpec((D_MODEL, TN_ADA), lambda j: (0, j)),
            pl.BlockSpec((1, TN_ADA), lambda j: (0, j)),
        ],
        out_specs=pl.BlockSpec((BATCH, TN_ADA), lambda j: (0, j)),
        out_shape=jax.ShapeDtypeStruct((BATCH, n), F32),
        compiler_params=pltpu.CompilerParams(
            dimension_semantics=("arbitrary",), vmem_limit_bytes=VMEM_LIMIT),
        name="ada",
    )(c, w_ada, b_ada.reshape(1, n))


def _next_tile_rows(step, n_rows):
    rows = n_rows // PRE_BLOCKS
    blk = jnp.minimum(step, PRE_BLOCKS - 1)
    return pl.ds(pl.multiple_of(blk * rows, rows), rows)


def _inproj_kernel(x_ref, shift_ref, scale_ref, nw_ref, w_ref, wg_ref, bg_ref,
                   main_ref, kt_ref, g_ref, h_scr):
    r = pl.program_id(0)
    j = pl.program_id(1)
    cur = (r + 1) % 2
    nxt = r % 2
    is_k = (j >= J_K0) & (j < J_K0 + NJ_K)
    gain = nw_ref[...] * (1.0 + scale_ref[0])
    shift = shift_ref[0]

    def norm_next_rows():
        rows = _next_tile_rows(j, TM_IN)
        h_scr[nxt, rows, :] = (_rms(x_ref[rows, :], gain) + shift).astype(BF16)

    @pl.when(r == 0)
    def _():
        norm_next_rows()

    @pl.when((r > 0) & (j == 0))
    def _():
        g_ref[...] = _dot(h_scr[cur], wg_ref[...]) + bg_ref[...]

    @pl.when((r > 0) & jnp.logical_not(is_k))
    def _():
        main_ref[...] = _dot(h_scr[cur], w_ref[...]).astype(BF16)
        norm_next_rows()

    @pl.when((r > 0) & is_k)
    def _():
        kt = (_dot(h_scr[cur], w_ref[...]) * (QK_DIM ** -0.5)).T
        for p in range(TM_IN // CHUNK):
            kt_ref[p] = kt[:, p * CHUNK:(p + 1) * CHUNK].astype(BF16)
        norm_next_rows()


def _inproj_call(x2d, ada3, norm_w, w_all, w_gate, b_gate):
    n_tiles = TOKENS // TM_IN
    tiles_per_seq = SEQ // TM_IN
    chunks_per_tile = TM_IN // CHUNK

    def next_tile(r):
        return jnp.minimum(r, n_tiles - 1)

    def this_tile(r):
        return jnp.maximum(r - 1, 0)

    def ada_map(k):
        return lambda r, j: ((next_tile(r) // tiles_per_seq) * 6 + k, 0, 0)

    def main_col(r, j):
        col = jnp.where(j < J_K0, j, jnp.maximum(j - NJ_K, J_K0 - 1))
        return jnp.where(r > 0, col, 0)

    return pl.pallas_call(
        _inproj_kernel,
        grid=(n_tiles + 1, NJ_A + NJ_BG),
        in_specs=[
            pl.BlockSpec((TM_IN, D_MODEL), lambda r, j: (next_tile(r), 0)),
            pl.BlockSpec((1, 1, D_MODEL), ada_map(0)),
            pl.BlockSpec((1, 1, D_MODEL), ada_map(1)),
            pl.BlockSpec((1, D_MODEL), lambda r, j: (0, 0)),
            pl.BlockSpec((D_MODEL, TN_IN), lambda r, j: (0, j)),
            pl.BlockSpec((D_MODEL, LANES), lambda r, j: (0, 0)),
            pl.BlockSpec((1, LANES), lambda r, j: (0, 0)),
        ],
        out_specs=[
            pl.BlockSpec((TM_IN, TN_IN), lambda r, j: (this_tile(r), main_col(r, j))),
            pl.BlockSpec((chunks_per_tile, M_QK, CHUNK), lambda r, j: (this_tile(r), 0, 0)),
            pl.BlockSpec((TM_IN, LANES), lambda r, j: (this_tile(r), 0)),
        ],
        out_shape=[
            jax.ShapeDtypeStruct((TOKENS, MAIN_COLS), BF16),
            jax.ShapeDtypeStruct((TOKENS // CHUNK, M_QK, CHUNK), BF16),
            jax.ShapeDtypeStruct((TOKENS, LANES), F32),
        ],
        scratch_shapes=[pltpu.VMEM((2, TM_IN, D_MODEL), BF16)],
        compiler_params=pltpu.CompilerParams(
            dimension_semantics=("arbitrary", "arbitrary"), vmem_limit_bytes=VMEM_LIMIT),
        name="inproj",
    )(x2d, ada3, ada3, norm_w, w_all, w_gate, b_gate)


def _log_sigmoid(x):
    return jnp.minimum(x, 0.0) - jnp.log1p(jnp.exp(-jnp.abs(x)))


def _mlstm_kernel(g_ref, q_ref, kt_ref, v_ref, o_ref, nw_ref, out_ref,
                  rows_scr, h_scr, cf_scr, cb_scr):
    L = CHUNK
    g = g_ref[0, 0]
    row_i = lax.broadcasted_iota(jnp.int32, (L, L), 0)
    col_i = lax.broadcasted_iota(jnp.int32, (L, L), 1)
    lower = col_i <= row_i
    upper = col_i >= row_i

    lf_f = _log_sigmoid(g[1])
    lf_b = _log_sigmoid(g[3])
    cum_f = jnp.dot(lf_f, upper.astype(F32), precision=lax.Precision.HIGHEST,
                    preferred_element_type=F32)
    cum_b = jnp.dot(lf_b, lower.astype(F32), precision=lax.Precision.HIGHEST,
                    preferred_element_type=F32)
    a_f = (g[0] - cum_f) * LOG2E
    a_b = (g[2] - cum_b) * LOG2E
    full = (N_CHUNKS, L)
    rows_scr[0] = lf_f * LOG2E
    rows_scr[1] = a_f
    rows_scr[2] = jnp.broadcast_to(cum_f[:, L - 1:L] * LOG2E, full)
    rows_scr[3] = jnp.broadcast_to(jnp.max(a_f, axis=1, keepdims=True), full)
    rows_scr[4] = lf_b * LOG2E
    rows_scr[5] = a_b
    rows_scr[6] = jnp.broadcast_to(cum_b[:, 0:1] * LOG2E, full)
    rows_scr[7] = jnp.broadcast_to(jnp.max(a_b, axis=1, keepdims=True), full)

    cf_scr[...] = jnp.zeros_like(cf_scr)
    cb_scr[...] = jnp.zeros_like(cb_scr)

    ones_block = jnp.ones((L, LANES), BF16)
    nw = nw_ref[...]

    def chunk(c, tri, base, c_scr, m_row):
        lf_row = rows_scr[base, pl.ds(c, 1), :]
        a_row = rows_scr[base + 1, pl.ds(c, 1), :]
        tot_row = rows_scr[base + 2, pl.ds(c, 1), :]
        amax_row = rows_scr[base + 3, pl.ds(c, 1), :]
        r0 = pl.multiple_of(c * L, L)
        q = q_ref[0, pl.ds(r0, L), :]
        kt = kt_ref[c]
        v_ext = jnp.concatenate([v_ref[0, pl.ds(r0, L), :], ones_block], axis=1)

        cum_col = jnp.sum(jnp.where(tri, lf_row, 0.0), axis=1, keepdims=True)
        a_mat = jnp.where(tri, a_row, -jnp.inf)
        cm_col = jnp.max(a_mat, axis=1, keepdims=True)
        g_mat = jnp.maximum(cm_col, m_row)
        s = (_dot(q, kt) * jnp.exp2(a_mat - g_mat)).astype(BF16)
        g_blk = g_mat[:, :LANES]
        inter = jnp.exp2(m_row[:, :LANES] - g_blk)
        floor = jnp.exp2(-(cum_col + g_blk))
        q_inter = (q.astype(F32) * inter).astype(BF16)
        c_state = c_scr[...]
        num = _dot(s, v_ext) + _dot(q_inter, c_state.astype(BF16))
        den = num[:, V_DIM:]
        r = 1.0 / jnp.maximum(jnp.abs(den), floor)
        h = num[:, :V_DIM] * jnp.tile(r, (1, V_DIM // LANES))

        g_row = jnp.maximum(m_row, amax_row)
        w = jnp.exp2(a_row - g_row)
        decay = jnp.exp2(m_row - g_row)
        ktw = (kt.astype(F32) * w).astype(BF16)
        c_scr[...] = jnp.tile(decay[:, :LANES], (1, V_EXT // LANES)) * c_state + _dot(ktw, v_ext)
        return h, tot_row + g_row

    def finish(c, h_sum):
        r0 = pl.multiple_of(c * L, L)
        hn = h_sum * lax.rsqrt(jnp.mean(h_sum * h_sum, axis=-1, keepdims=True) + EPS) * nw
        out_ref[0, pl.ds(r0, L), :] = _sigmoid(o_ref[0, pl.ds(r0, L), :]) * hn.astype(BF16)

    def step(i, carry, finalize):
        m_f, m_b = carry
        cf = i
        cb = N_CHUNKS - 1 - i
        h_f, m_f = chunk(cf, lower, 0, cf_scr, m_f)
        h_b, m_b = chunk(cb, upper, 4, cb_scr, m_b)
        rf = pl.multiple_of(cf * L, L)
        rb = pl.multiple_of(cb * L, L)
        if finalize:
            finish(cf, h_f + h_scr[pl.ds(rf, L), :])
            finish(cb, h_b + h_scr[pl.ds(rb, L), :])
        else:
            h_scr[pl.ds(rf, L), :] = h_f
            h_scr[pl.ds(rb, L), :] = h_b
        return m_f, m_b

    m0 = jnp.zeros((1, L), F32)
    half = N_CHUNKS // 2
    carry = lax.fori_loop(0, half, functools.partial(step, finalize=False), (m0, m0),
                          unroll=MLSTM_UNROLL)
    lax.fori_loop(half, N_CHUNKS, functools.partial(step, finalize=True), carry,
                  unroll=MLSTM_UNROLL)


def _mlstm_call(gates, main3, kt, norm_w_heads):
    q_blk = MAIN_Q // QK_DIM
    v_blk = MAIN_V // V_DIM
    o_blk = MAIN_O // V_DIM
    return pl.pallas_call(
        _mlstm_kernel,
        grid=(BATCH, M_HEADS),
        in_specs=[
            pl.BlockSpec((1, 1, 4, N_CHUNKS, CHUNK), lambda b, h: (b, h, 0, 0, 0)),
            pl.BlockSpec((1, SEQ, QK_DIM), lambda b, h: (b, 0, q_blk + h)),
            pl.BlockSpec((N_CHUNKS, QK_DIM, CHUNK), lambda b, h: (b, h, 0)),
            pl.BlockSpec((1, SEQ, V_DIM), lambda b, h: (b, 0, v_blk + h)),
            pl.BlockSpec((1, SEQ, V_DIM), lambda b, h: (b, 0, o_blk + h)),
            pl.BlockSpec((1, V_DIM), lambda b, h: (0, h)),
        ],
        out_specs=pl.BlockSpec((1, SEQ, V_DIM), lambda b, h: (b, 0, h)),
        out_shape=jax.ShapeDtypeStruct((BATCH, SEQ, M_V), BF16),
        scratch_shapes=[
            pltpu.VMEM((8, N_CHUNKS, CHUNK), F32),
            pltpu.VMEM((SEQ, V_DIM), F32),
            pltpu.VMEM((QK_DIM, V_EXT), F32),
            pltpu.VMEM((QK_DIM, V_EXT), F32),
        ],
        compiler_params=pltpu.CompilerParams(
            dimension_semantics=("arbitrary", "arbitrary"), vmem_limit_bytes=VMEM_LIMIT),
        name="mlstm",
    )(gates, main3, kt, main3, main3, norm_w_heads)


def _mix_kernel(x_ref, cb_ref, cc_ref, cx_ref, ccp_ref, cxp_ref, ccn_ref, cxn_ref,
                gc_ref, gm_ref, hm_ref, gate1_ref, convw_ref, wconv_ref, wml_ref, wo_ref,
                x1_ref):
    i = pl.program_id(0)
    tm = TM_MIX
    tiles_per_seq = SEQ // tm
    pos = i % tiles_per_seq

    u = cc_ref[...].astype(F32) * cx_ref[...].astype(F32)
    last = BF16_SUBLANES - 1
    u_prev = (ccp_ref[last:last + 1, :].astype(F32) * cxp_ref[last:last + 1, :].astype(F32))
    u_next = ccn_ref[0:1, :].astype(F32) * cxn_ref[0:1, :].astype(F32)
    u_prev = jnp.where(pos == 0, 0.0, u_prev)
    u_next = jnp.where(pos == tiles_per_seq - 1, 0.0, u_next)
    row = lax.broadcasted_iota(jnp.int32, (tm, 1), 0)
    u_m1 = jnp.where(row == 0, u_prev, pltpu.roll(u, 1, 0))
    u_p1 = jnp.where(row == tm - 1, u_next, pltpu.roll(u, tm - 1, 0))
    w = convw_ref[...]
    conv = w[0:1, :] * u_m1 + w[1:2, :] * u + w[2:3, :] * u_p1
    feat = (cb_ref[...].astype(F32) * conv).astype(BF16)
    y_conv = _dot(feat, wconv_ref[...])
    y_mlstm = _dot(hm_ref[...], wml_ref[...])
    merged = (_sigmoid(gc_ref[...].astype(F32)) * y_conv
              + _sigmoid(gm_ref[...].astype(F32)) * y_mlstm).astype(BF16)
    x1_ref[...] = x_ref[...] + gate1_ref[0] * _dot(merged, wo_ref[...])


def _mix_call(x2d, main, hm, ada3, conv_w, w_conv_out, w_mlstm_out, w_o):
    tm = TM_MIX
    tiles_per_seq = SEQ // tm
    halo = BF16_SUBLANES
    n_halo = TOKENS // halo
    per = tm // halo

    def col(k):
        return lambda i: (i, k)

    def prev_map(k):
        return lambda i: (jnp.maximum(i * per - 1, 0), k)

    def next_map(k):
        return lambda i: (jnp.minimum((i + 1) * per, n_halo - 1), k)

    def resident(shape):
        return pl.BlockSpec(shape, lambda i: (0, 0), pipeline_mode=pl.Buffered(1))

    cw = CONV_WIDTH
    return pl.pallas_call(
        _mix_kernel,
        grid=(TOKENS // tm,),
        in_specs=[
            pl.BlockSpec((tm, D_MODEL), lambda i: (i, 0)),
            pl.BlockSpec((tm, cw), col(0)),
            pl.BlockSpec((tm, cw), col(1)),
            pl.BlockSpec((tm, cw), col(2)),
            pl.BlockSpec((halo, cw), prev_map(1)),
            pl.BlockSpec((halo, cw), prev_map(2)),
            pl.BlockSpec((halo, cw), next_map(1)),
            pl.BlockSpec((halo, cw), next_map(2)),
            pl.BlockSpec((tm, D_MODEL), col(MAIN_GC // D_MODEL)),
            pl.BlockSpec((tm, D_MODEL), col(MAIN_GM // D_MODEL)),
            pl.BlockSpec((tm, M_V), lambda i: (i, 0)),
            pl.BlockSpec((1, 1, D_MODEL), lambda i: ((i // tiles_per_seq) * 6 + 2, 0, 0)),
            resident((3, cw)),
            resident((cw, D_MODEL)),
            resident((M_V, D_MODEL)),
            resident((D_MODEL, D_MODEL)),
        ],
        out_specs=pl.BlockSpec((tm, D_MODEL), lambda i: (i, 0)),
        out_shape=jax.ShapeDtypeStruct((TOKENS, D_MODEL), F32),
        compiler_params=pltpu.CompilerParams(
            dimension_semantics=("arbitrary",), vmem_limit_bytes=VMEM_LIMIT),
        name="mix",
    )(x2d, main, main, main, main, main, main, main, main, main, hm, ada3,
      conv_w, w_conv_out, w_mlstm_out, w_o)


def _ffn_up_kernel(x1_ref, shift_ref, scale_ref, nw_ref, wg_ref, wu_ref, act_ref, h_scr):
    r = pl.program_id(0)
    f = pl.program_id(1)
    cur = (r + 1) % 2
    nxt = r % 2
    gain = nw_ref[...] * (1.0 + scale_ref[0])
    shift = shift_ref[0]

    def norm_next_rows():
        rows = _next_tile_rows(f, TM_UP)
        h_scr[nxt, rows, :] = (_rms(x1_ref[rows, :], gain) + shift).astype(BF16)

    @pl.when(r == 0)
    def _():
        norm_next_rows()

    @pl.when(r > 0)
    def _():
        h = h_scr[cur]
        gt = _dot(h, wg_ref[...])
        up = _dot(h, wu_ref[...])
        act_ref[...] = (gt * _sigmoid(gt) * up).astype(BF16)
        norm_next_rows()


def _ffn_up_call(x1, ada3, norm_w, w_gate_up):
    n_tiles = TOKENS // TM_UP
    tiles_per_seq = SEQ // TM_UP

    def next_tile(r):
        return jnp.minimum(r, n_tiles - 1)

    def ada_map(k):
        return lambda r, f: ((next_tile(r) // tiles_per_seq) * 6 + k, 0, 0)

    return pl.pallas_call(
        _ffn_up_kernel,
        grid=(n_tiles + 1, NF_FFN),
        in_specs=[
            pl.BlockSpec((TM_UP, D_MODEL), lambda r, f: (next_tile(r), 0)),
            pl.BlockSpec((1, 1, D_MODEL), ada_map(3)),
            pl.BlockSpec((1, 1, D_MODEL), ada_map(4)),
            pl.BlockSpec((1, D_MODEL), lambda r, f: (0, 0)),
            pl.BlockSpec((D_MODEL, TF_FFN), lambda r, f: (0, f)),
            pl.BlockSpec((D_MODEL, TF_FFN), lambda r, f: (0, NF_FFN + f)),
        ],
        out_specs=pl.BlockSpec((TM_UP, TF_FFN),
                               lambda r, f: (jnp.maximum(r - 1, 0), jnp.where(r > 0, f, 0))),
        out_shape=jax.ShapeDtypeStruct((TOKENS, FFN_HIDDEN), BF16),
        scratch_shapes=[pltpu.VMEM((2, TM_UP, D_MODEL), BF16)],
        compiler_params=pltpu.CompilerParams(
            dimension_semantics=("arbitrary", "arbitrary"), vmem_limit_bytes=VMEM_LIMIT),
        name="ffn_up",
    )(x1, ada3, ada3, norm_w, w_gate_up, w_gate_up)


def _ffn_down_kernel(act_ref, x1_ref, gate_ref, fnw_ref, wd_ref, out_ref):
    x2 = x1_ref[...] + gate_ref[0] * _dot(act_ref[...], wd_ref[...])
    out_ref[...] = _rms(x2, fnw_ref[...])


def _ffn_down_call(act, x1, ada3, final_norm_w, w_down):
    tm = TM_DOWN
    tiles_per_seq = SEQ // tm
    return pl.pallas_call(
        _ffn_down_kernel,
        grid=(TOKENS // tm,),
        in_specs=[
            pl.BlockSpec((tm, FFN_HIDDEN), lambda i: (i, 0)),
            pl.BlockSpec((tm, D_MODEL), lambda i: (i, 0)),
            pl.BlockSpec((1, 1, D_MODEL), lambda i: ((i // tiles_per_seq) * 6 + 5, 0, 0)),
            pl.BlockSpec((1, D_MODEL), lambda i: (0, 0)),
            pl.BlockSpec((FFN_HIDDEN, D_MODEL), lambda i: (0, 0), pipeline_mode=pl.Buffered(1)),
        ],
        out_specs=pl.BlockSpec((tm, D_MODEL), lambda i: (i, 0)),
        out_shape=jax.ShapeDtypeStruct((TOKENS, D_MODEL), F32),
        compiler_params=pltpu.CompilerParams(
            dimension_semantics=("arbitrary",), vmem_limit_bytes=VMEM_LIMIT),
        name="ffn_down",
    )(act, x1, ada3, final_norm_w, w_down)


def kernel(x, c, w_ada, b_ada, norm1_w, w_in_mix, conv_w, mlstm_gate_bias, mlstm_norm_w,
           w_conv_out, w_mlstm_out, w_o, norm2_w, w_gate_up, w_down, final_norm_w):
    assert x.shape == (BATCH, SEQ, D_MODEL) and w_ada.shape[0] == 1
    x2d = x.reshape(TOKENS, D_MODEL)

    ada = _ada_call(c, w_ada[0], b_ada[0])
    ada3 = ada.reshape(BATCH * 6, 1, D_MODEL)

    w_in = w_in_mix[0]
    w_all = lax.dynamic_update_slice(w_in.astype(BF16), w_in[:, OFF_BG:].astype(BF16), (0, OFF_G))
    w_gate = jnp.pad(w_in[:, OFF_G:OFF_BG], ((0, 0), (0, LANES - N_GATES))).astype(BF16)
    b_gate = jnp.pad(mlstm_gate_bias[0], (0, LANES - N_GATES)).reshape(1, LANES)

    main, kt, gpre = _inproj_call(x2d, ada3, norm1_w[0].reshape(1, D_MODEL),
                                  w_all, w_gate, b_gate)

    gates = gpre[:, :N_GATES].reshape(BATCH, N_CHUNKS, CHUNK, 4, M_HEADS)
    gates = gates.transpose(0, 4, 3, 1, 2)
    hm = _mlstm_call(gates, main.reshape(BATCH, SEQ, MAIN_COLS), kt,
                     mlstm_norm_w[0].reshape(1, M_V))

    x1 = _mix_call(x2d, main, hm.reshape(TOKENS, M_V), ada3, conv_w[0],
                   w_conv_out[0].astype(BF16), w_mlstm_out[0].astype(BF16), w_o[0].astype(BF16))

    act = _ffn_up_call(x1, ada3, norm2_w[0].reshape(1, D_MODEL), w_gate_up[0].astype(BF16))
    out = _ffn_down_call(act, x1, ada3, final_norm_w.reshape(1, D_MODEL), w_down[0].astype(BF16))
    return out.reshape(BATCH, SEQ, D_MODEL)
```

```python
import functools
import math

import jax
import jax.numpy as jnp
from jax import lax
from jax.experimental import pallas as pl
from jax.experimental.pallas import tpu as pltpu

D_MODEL = 2048
BATCH = 8
SEQ = 4096
TOKENS = BATCH * SEQ
EPS = 1e-6
CONV_WIDTH = D_MODEL // 2
M_HEADS = 8
QK_DIM = D_MODEL // 16
V_DIM = D_MODEL // 8
M_QK = M_HEADS * QK_DIM
M_V = M_HEADS * V_DIM
N_GATES = 4 * M_HEADS
FFN_HIDDEN = int(math.ceil((8 * D_MODEL / 3) / 256) * 256)

OFF_CB, OFF_CC, OFF_CX = 0, CONV_WIDTH, 2 * CONV_WIDTH
OFF_Q = 3 * CONV_WIDTH
OFF_K = OFF_Q + M_QK
OFF_V = OFF_K + M_QK
OFF_O = OFF_V + M_V
OFF_G = OFF_O + M_V
OFF_BG = OFF_G + N_GATES
IN_COLS = OFF_BG + 2 * D_MODEL

MAIN_COLS = IN_COLS - M_QK - N_GATES
MAIN_Q = 3 * CONV_WIDTH
MAIN_V = MAIN_Q + M_QK
MAIN_O = MAIN_V + M_V
MAIN_GC = MAIN_O + M_V
MAIN_GM = MAIN_GC + D_MODEL

LANES = 128
BF16_SUBLANES = 16
VMEM_LIMIT = 56 * 1024 * 1024

CHUNK = 256
N_CHUNKS = SEQ // CHUNK
MLSTM_UNROLL = 8
V_EXT = V_DIM + LANES

TM_IN = 1024
TN_IN = 1024
NJ_A = OFF_G // TN_IN
J_K0 = OFF_K // TN_IN
NJ_K = M_QK // TN_IN
NJ_BG = 2 * D_MODEL // TN_IN
assert NJ_K == 1 and OFF_G % TN_IN == 0
PRE_BLOCKS = 8
TM_MIX = 256
TM_UP = 1024
TM_DOWN = 256
TF_FFN = 512
NF_FFN = FFN_HIDDEN // TF_FFN
TN_ADA = 1024
ROW_BLOCK = 128

F32 = jnp.float32
BF16 = jnp.bfloat16
LOG2E = math.log2(math.e)


def _dot(a, b):
    return jnp.dot(a, b, preferred_element_type=F32)


def _sigmoid(x):
    return 1.0 / (1.0 + jnp.exp(-x))


def _rms(x, norm_w):
    return x * lax.rsqrt(jnp.mean(x * x, axis=-1, keepdims=True) + EPS) * norm_w


def _row_blocks(n_rows, body):
    def step(r, carry):
        body(pl.ds(pl.multiple_of(r * ROW_BLOCK, ROW_BLOCK), ROW_BLOCK))
        return carry
    lax.fori_loop(0, n_rows // ROW_BLOCK, step, 0)


def _rms_modulate_into(dst_ref, x_ref, norm_w, scale, shift):
    gain = norm_w * (1.0 + scale)

    def body(rows):
        dst_ref[rows, :] = (_rms(x_ref[rows, :], gain) + shift).astype(BF16)
    _row_blocks(x_ref.shape[0], body)


def _ada_kernel(c_ref, w_ref, b_ref, o_ref):
    c = c_ref[...]
    c_act = (c * _sigmoid(c)).astype(BF16)
    o_ref[...] = _dot(c_act, w_ref[...].astype(BF16)) + b_ref[...]


def _ada_call(c, w_ada, b_ada):
    n = w_ada.shape[1]
    return pl.pallas_call(
        _ada_kernel,
        grid=(n // TN_ADA,),
        in_specs=[
            pl.BlockSpec((BATCH, D_MODEL), lambda j: (0, 0)),
            pl.BlockSpec((D_MODEL, TN_ADA), lambda j: (0, j)),
            pl.BlockSpec((1, TN_ADA), lambda j: (0, j)),
        ],
        out_specs=pl.BlockSpec((BATCH, TN_ADA), lambda j: (0, j)),
        out_shape=jax.ShapeDtypeStruct((BATCH, n), F32),
        compiler_params=pltpu.CompilerParams(
            dimension_semantics=("arbitrary",), vmem_limit_bytes=VMEM_LIMIT),
        name="ada",
    )(c, w_ada, b_ada.reshape(1, n))


def _next_tile_rows(step, n_rows):
    rows = n_rows // PRE_BLOCKS
    blk = jnp.minimum(step, PRE_BLOCKS - 1)
    return pl.ds(pl.multiple_of(blk * rows, rows), rows)


def _inproj_kernel(x_ref, shift_ref, scale_ref, nw_ref, w_ref, wg_ref, bg_ref,
                   main_ref, kt_ref, g_ref, h_scr):
    r = pl.program_id(0)
    j = pl.program_id(1)
    cur = (r + 1) % 2
    nxt = r % 2
    is_k = (j >= J_K0) & (j < J_K0 + NJ_K)
    gain = nw_ref[...] * (1.0 + scale_ref[0])
    shift = shift_ref[0]

    def norm_next_rows():
        rows = _next_tile_rows(j, TM_IN)
        h_scr[nxt, rows, :] = (_rms(x_ref[rows, :], gain) + shift).astype(BF16)

    @pl.when(r == 0)
    def _():
        norm_next_rows()

    @pl.when((r > 0) & (j == 0))
    def _():
        g_ref[...] = _dot(h_scr[cur], wg_ref[...]) + bg_ref[...]

    @pl.when((r > 0) & jnp.logical_not(is_k))
    def _():
        main_ref[...] = _dot(h_scr[cur], w_ref[...]).astype(BF16)
        norm_next_rows()

    @pl.when((r > 0) & is_k)
    def _():
        kt = (_dot(h_scr[cur], w_ref[...]) * (QK_DIM ** -0.5)).T
        for p in range(TM_IN // CHUNK):
            kt_ref[p] = kt[:, p * CHUNK:(p + 1) * CHUNK].astype(BF16)
        norm_next_rows()


def _inproj_call(x2d, ada3, norm_w, w_all, w_gate, b_gate):
    n_tiles = TOKENS // TM_IN
    tiles_per_seq = SEQ // TM_IN
    chunks_per_tile = TM_IN // CHUNK

    def next_tile(r):
        return jnp.minimum(r, n_tiles - 1)

    def this_tile(r):
        return jnp.maximum(r - 1, 0)

    def ada_map(k):
        return lambda r, j: ((next_tile(r) // tiles_per_seq) * 6 + k, 0, 0)

    def main_col(r, j):
        col = jnp.where(j < J_K0, j, jnp.maximum(j - NJ_K, J_K0 - 1))
        return jnp.where(r > 0, col, 0)

    return pl.pallas_call(
        _inproj_kernel,
        grid=(n_tiles + 1, NJ_A + NJ_BG),
        in_specs=[
            pl.BlockSpec((TM_IN, D_MODEL), lambda r, j: (next_tile(r), 0)),
            pl.BlockSpec((1, 1, D_MODEL), ada_map(0)),
            pl.BlockSpec((1, 1, D_MODEL), ada_map(1)),
            pl.BlockSpec((1, D_MODEL), lambda r, j: (0, 0)),
            pl.BlockSpec((D_MODEL, TN_IN), lambda r, j: (0, j)),
            pl.BlockSpec((D_MODEL, LANES), lambda r, j: (0, 0)),
            pl.BlockSpec((1, LANES), lambda r, j: (0, 0)),
        ],
        out_specs=[
            pl.BlockSpec((TM_IN, TN_IN), lambda r, j: (this_tile(r), main_col(r, j))),
            pl.BlockSpec((chunks_per_tile, M_QK, CHUNK), lambda r, j: (this_tile(r), 0, 0)),
            pl.BlockSpec((TM_IN, LANES), lambda r, j: (this_tile(r), 0)),
        ],
        out_shape=[
            jax.ShapeDtypeStruct((TOKENS, MAIN_COLS), BF16),
            jax.ShapeDtypeStruct((TOKENS // CHUNK, M_QK, CHUNK), BF16),
            jax.ShapeDtypeStruct((TOKENS, LANES), F32),
        ],
        scratch_shapes=[pltpu.VMEM((2, TM_IN, D_MODEL), BF16)],
        compiler_params=pltpu.CompilerParams(
            dimension_semantics=("arbitrary", "arbitrary"), vmem_limit_bytes=VMEM_LIMIT),
        name="inproj",
    )(x2d, ada3, ada3, norm_w, w_all, w_gate, b_gate)


def _log_sigmoid(x):
    return jnp.minimum(x, 0.0) - jnp.log1p(jnp.exp(-jnp.abs(x)))


def _mlstm_kernel(g_ref, q_ref, kt_ref, v_ref, o_ref, nw_ref, out_ref,
                  rows_scr, h_scr, cf_scr, cb_scr):
    L = CHUNK
    g = g_ref[0, 0]
    row_i = lax.broadcasted_iota(jnp.int32, (L, L), 0)
    col_i = lax.broadcasted_iota(jnp.int32, (L, L), 1)
    lower = col_i <= row_i
    upper = col_i >= row_i

    lf_f = _log_sigmoid(g[1])
    lf_b = _log_sigmoid(g[3])
    cum_f = jnp.dot(lf_f, upper.astype(F32), precision=lax.Precision.HIGHEST,
                    preferred_element_type=F32)
    cum_b = jnp.dot(lf_b, lower.astype(F32), precision=lax.Precision.HIGHEST,
                    preferred_element_type=F32)
    a_f = (g[0] - cum_f) * LOG2E
    a_b = (g[2] - cum_b) * LOG2E
    full = (N_CHUNKS, L)
    rows_scr[0] = lf_f * LOG2E
    rows_scr[1] = a_f
    rows_scr[2] = jnp.broadcast_to(cum_f[:, L - 1:L] * LOG2E, full)
    rows_scr[3] = jnp.broadcast_to(jnp.max(a_f, axis=1, keepdims=True), full)
    rows_scr[4] = lf_b * LOG2E
    rows_scr[5] = a_b
    rows_scr[6] = jnp.broadcast_to(cum_b[:, 0:1] * LOG2E, full)
    rows_scr[7] = jnp.broadcast_to(jnp.max(a_b, axis=1, keepdims=True), full)

    cf_scr[...] = jnp.zeros_like(cf_scr)
    cb_scr[...] = jnp.zeros_like(cb_scr)

    ones_block = jnp.ones((L, LANES), BF16)
    nw = nw_ref[...]

    def chunk(c, tri, base, c_scr, m_row):
        lf_row = rows_scr[base, pl.ds(c, 1), :]
        a_row = rows_scr[base + 1, pl.ds(c, 1), :]
        tot_row = rows_scr[base + 2, pl.ds(c, 1), :]
        amax_row = rows_scr[base + 3, pl.ds(c, 1), :]
        r0 = pl.multiple_of(c * L, L)
        q = q_ref[0, pl.ds(r0, L), :]
        kt = kt_ref[c]
        v_ext = jnp.concatenate([v_ref[0, pl.ds(r0, L), :], ones_block], axis=1)

        cum_col = jnp.sum(jnp.where(tri, lf_row, 0.0), axis=1, keepdims=True)
        a_mat = jnp.where(tri, a_row, -jnp.inf)
        cm_col = jnp.max(a_mat, axis=1, keepdims=True)
        g_mat = jnp.maximum(cm_col, m_row)
        s = (_dot(q, kt) * jnp.exp2(a_mat - g_mat)).astype(BF16)
        g_blk = g_mat[:, :LANES]
        inter = jnp.exp2(m_row[:, :LANES] - g_blk)
        floor = jnp.exp2(-(cum_col + g_blk))
        q_inter = (q.astype(F32) * inter).astype(BF16)
        c_state = c_scr[...]
        num = _dot(s, v_ext) + _dot(q_inter, c_state.astype(BF16))
        den = num[:, V_DIM:]
        r = 1.0 / jnp.maximum(jnp.abs(den), floor)
        h = num[:, :V_DIM] * jnp.tile(r, (1, V_DIM // LANES))

        g_row = jnp.maximum(m_row, amax_row)
        w = jnp.exp2(a_row - g_row)
        decay = jnp.exp2(m_row - g_row)
        ktw = (kt.astype(F32) * w).astype(BF16)
        c_scr[...] = jnp.tile(decay[:, :LANES], (1, V_EXT // LANES)) * c_state + _dot(ktw, v_ext)
        return h, tot_row + g_row

    def finish(c, h_sum):
        r0 = pl.multiple_of(c * L, L)
        hn = h_sum * lax.rsqrt(jnp.mean(h_sum * h_sum, axis=-1, keepdims=True) + EPS) * nw
        out_ref[0, pl.ds(r0, L), :] = _sigmoid(o_ref[0, pl.ds(r0, L), :]) * hn.astype(BF16)

    def step(i, carry, finalize):
        m_f, m_b = carry
        cf = i
        cb = N_CHUNKS - 1 - i
        h_f, m_f = chunk(cf, lower, 0, cf_scr, m_f)
        h_b, m_b = chunk(cb, upper, 4, cb_scr, m_b)
        rf = pl.multiple_of(cf * L, L)
        rb = pl.multiple_of(cb * L, L)
        if finalize:
            finish(cf, h_f + h_scr[pl.ds(rf, L), :])
            finish(cb, h_b + h_scr[pl.ds(rb, L), :])
        else:
            h_scr[pl.ds(rf, L), :] = h_f
            h_scr[pl.ds(rb, L), :] = h_b
        return m_f, m_b

    m0 = jnp.zeros((1, L), F32)
    half = N_CHUNKS // 2
    carry = lax.fori_loop(0, half, functools.partial(step, finalize=False), (m0, m0),
                          unroll=MLSTM_UNROLL)
    lax.fori_loop(half, N_CHUNKS, functools.partial(step, finalize=True), carry,
                  unroll=MLSTM_UNROLL)


def _mlstm_call(gates, main3, kt, norm_w_heads):
    q_blk = MAIN_Q // QK_DIM
    v_blk = MAIN_V // V_DIM
    o_blk = MAIN_O // V_DIM
    return pl.pallas_call(
        _mlstm_kernel,
        grid=(BATCH, M_HEADS),
        in_specs=[
            pl.BlockSpec((1, 1, 4, N_CHUNKS, CHUNK), lambda b, h: (b, h, 0, 0, 0)),
            pl.BlockSpec((1, SEQ, QK_DIM), lambda b, h: (b, 0, q_blk + h)),
            pl.BlockSpec((N_CHUNKS, QK_DIM, CHUNK), lambda b, h: (b, h, 0)),
            pl.BlockSpec((1, SEQ, V_DIM), lambda b, h: (b, 0, v_blk + h)),
            pl.BlockSpec((1, SEQ, V_DIM), lambda b, h: (b, 0, o_blk + h)),
            pl.BlockSpec((1, V_DIM), lambda b, h: (0, h)),
        ],
        out_specs=pl.BlockSpec((1, SEQ, V_DIM), lambda b, h: (b, 0, h)),
        out_shape=jax.ShapeDtypeStruct((BATCH, SEQ, M_V), BF16),
        scratch_shapes=[
            pltpu.VMEM((8, N_CHUNKS, CHUNK), F32),
            pltpu.VMEM((SEQ, V_DIM), F32),
            pltpu.VMEM((QK_DIM, V_EXT), F32),
            pltpu.VMEM((QK_DIM, V_EXT), F32),
        ],
        compiler_params=pltpu.CompilerParams(
            dimension_semantics=("arbitrary", "arbitrary"), vmem_limit_bytes=VMEM_LIMIT),
        name="mlstm",
    )(gates, main3, kt, main3, main3, norm_w_heads)


def _mix_kernel(x_ref, cb_ref, cc_ref, cx_ref, ccp_ref, cxp_ref, ccn_ref, cxn_ref,
                gc_ref, gm_ref, hm_ref, gate1_ref, convw_ref, wconv_ref, wml_ref, wo_ref,
                x1_ref):
    i = pl.program_id(0)
    tm = TM_MIX
    tiles_per_seq = SEQ // tm
    pos = i % tiles_per_seq

    u = cc_ref[...].astype(F32) * cx_ref[...].astype(F32)
    last = BF16_SUBLANES - 1
    u_prev = (ccp_ref[last:last + 1, :].astype(F32) * cxp_ref[last:last + 1, :].astype(F32))
    u_next = ccn_ref[0:1, :].astype(F32) * cxn_ref[0:1, :].astype(F32)
    u_prev = jnp.where(pos == 0, 0.0, u_prev)
    u_next = jnp.where(pos == tiles_per_seq - 1, 0.0, u_next)
    row = lax.broadcasted_iota(jnp.int32, (tm, 1), 0)
    u_m1 = jnp.where(row == 0, u_prev, pltpu.roll(u, 1, 0))
    u_p1 = jnp.where(row == tm - 1, u_next, pltpu.roll(u, tm - 1, 0))
    w = convw_ref[...]
    conv = w[0:1, :] * u_m1 + w[1:2, :] * u + w[2:3, :] * u_p1
    feat = (cb_ref[...].astype(F32) * conv).astype(BF16)
    y_conv = _dot(feat, wconv_ref[...])
    y_mlstm = _dot(hm_ref[...], wml_ref[...])
    merged = (_sigmoid(gc_ref[...].astype(F32)) * y_conv
              + _sigmoid(gm_ref[...].astype(F32)) * y_mlstm).astype(BF16)
    x1_ref[...] = x_ref[...] + gate1_ref[0] * _dot(merged, wo_ref[...])


def _mix_call(x2d, main, hm, ada3, conv_w, w_conv_out, w_mlstm_out, w_o):
    tm = TM_MIX
    tiles_per_seq = SEQ // tm
    halo = BF16_SUBLANES
    n_halo = TOKENS // halo
    per = tm // halo

    def col(k):
        return lambda i: (i, k)

    def prev_map(k):
        return lambda i: (jnp.maximum(i * per - 1, 0), k)

    def next_map(k):
        return lambda i: (jnp.minimum((i + 1) * per, n_halo - 1), k)

    def resident(shape):
        return pl.BlockSpec(shape, lambda i: (0, 0), pipeline_mode=pl.Buffered(1))

    cw = CONV_WIDTH
    return pl.pallas_call(
        _mix_kernel,
        grid=(TOKENS // tm,),
        in_specs=[
            pl.BlockSpec((tm, D_MODEL), lambda i: (i, 0)),
            pl.BlockSpec((tm, cw), col(0)),
            pl.BlockSpec((tm, cw), col(1)),
            pl.BlockSpec((tm, cw), col(2)),
            pl.BlockSpec((halo, cw), prev_map(1)),
            pl.BlockSpec((halo, cw), prev_map(2)),
            pl.BlockSpec((halo, cw), next_map(1)),
            pl.BlockSpec((halo, cw), next_map(2)),
            pl.BlockSpec((tm, D_MODEL), col(MAIN_GC // D_MODEL)),
            pl.BlockSpec((tm, D_MODEL), col(MAIN_GM // D_MODEL)),
            pl.BlockSpec((tm, M_V), lambda i: (i, 0)),
            pl.BlockSpec((1, 1, D_MODEL), lambda i: ((i // tiles_per_seq) * 6 + 2, 0, 0)),
            resident((3, cw)),
            resident((cw, D_MODEL)),
            resident((M_V, D_MODEL)),
            resident((D_MODEL, D_MODEL)),
        ],
        out_specs=pl.BlockSpec((tm, D_MODEL), lambda i: (i, 0)),
        out_shape=jax.ShapeDtypeStruct((TOKENS, D_MODEL), F32),
        compiler_params=pltpu.CompilerParams(
            dimension_semantics=("arbitrary",), vmem_limit_bytes=VMEM_LIMIT),
        name="mix",
    )(x2d, main, main, main, main, main, main, main, main, main, hm, ada3,
      conv_w, w_conv_out, w_mlstm_out, w_o)


def _ffn_up_kernel(x1_ref, shift_ref, scale_ref, nw_ref, wg_ref, wu_ref, act_ref, h_scr):
    r = pl.program_id(0)
    f = pl.program_id(1)
    cur = (r + 1) % 2
    nxt = r % 2
    gain = nw_ref[...] * (1.0 + scale_ref[0])
    shift = shift_ref[0]

    def norm_next_rows():
        rows = _next_tile_rows(f, TM_UP)
        h_scr[nxt, rows, :] = (_rms(x1_ref[rows, :], gain) + shift).astype(BF16)

    @pl.when(r == 0)
    def _():
        norm_next_rows()

    @pl.when(r > 0)
    def _():
        h = h_scr[cur]
        gt = _dot(h, wg_ref[...])
        up = _dot(h, wu_ref[...])
        act_ref[...] = (gt * _sigmoid(gt) * up).astype(BF16)
        norm_next_rows()


def _ffn_up_call(x1, ada3, norm_w, w_gate_up):
    n_tiles = TOKENS // TM_UP
    tiles_per_seq = SEQ // TM_UP

    def next_tile(r):
        return jnp.minimum(r, n_tiles - 1)

    def ada_map(k):
        return lambda r, f: ((next_tile(r) // tiles_per_seq) * 6 + k, 0, 0)

    return pl.pallas_call(
        _ffn_up_kernel,
        grid=(n_tiles + 1, NF_FFN),
        in_specs=[
            pl.BlockSpec((TM_UP, D_MODEL), lambda r, f: (next_tile(r), 0)),
            pl.BlockSpec((1, 1, D_MODEL), ada_map(3)),
            pl.BlockSpec((1, 1, D_MODEL), ada_map(4)),
            pl.BlockSpec((1, D_MODEL), lambda r, f: (0, 0)),
            pl.BlockSpec((D_MODEL, TF_FFN), lambda r, f: (0, f)),
            pl.BlockSpec((D_MODEL, TF_FFN), lambda r, f: (0, NF_FFN + f)),
        ],
        out_specs=pl.BlockSpec((TM_UP, TF_FFN),
                               lambda r, f: (jnp.maximum(r - 1, 0), jnp.where(r > 0, f, 0))),
        out_shape=jax.ShapeDtypeStruct((TOKENS, FFN_HIDDEN), BF16),
        scratch_shapes=[pltpu.VMEM((2, TM_UP, D_MODEL), BF16)],
        compiler_params=pltpu.CompilerParams(
            dimension_semantics=("arbitrary", "arbitrary"), vmem_limit_bytes=VMEM_LIMIT),
        name="ffn_up",
    )(x1, ada3, ada3, norm_w, w_gate_up, w_gate_up)


def _ffn_down_kernel(act_ref, x1_ref, gate_ref, fnw_ref, wd_ref, out_ref):
    x2 = x1_ref[...] + gate_ref[0] * _dot(act_ref[...], wd_ref[...])
    out_ref[...] = _rms(x2, fnw_ref[...])


def _ffn_down_call(act, x1, ada3, final_norm_w, w_down):
    tm = TM_DOWN
    tiles_per_seq = SEQ // tm
    return pl.pallas_call(
        _ffn_down_kernel,
        grid=(TOKENS // tm,),
        in_specs=[
            pl.BlockSpec((tm, FFN_HIDDEN), lambda i: (i, 0)),
            pl.BlockSpec((tm, D_MODEL), lambda i: (i, 0)),
            pl.BlockSpec((1, 1, D_MODEL), lambda i: ((i // tiles_per_seq) * 6 + 5, 0, 0)),
            pl.BlockSpec((1, D_MODEL), lambda i: (0, 0)),
            pl.BlockSpec((FFN_HIDDEN, D_MODEL), lambda i: (0, 0), pipeline_mode=pl.Buffered(1)),
        ],
        out_specs=pl.BlockSpec((tm, D_MODEL), lambda i: (i, 0)),
        out_shape=jax.ShapeDtypeStruct((TOKENS, D_MODEL), F32),
        compiler_params=pltpu.CompilerParams(
            dimension_semantics=("arbitrary",), vmem_limit_bytes=VMEM_LIMIT),
        name="ffn_down",
    )(act, x1, ada3, final_norm_w, w_down)


def kernel(x, c, w_ada, b_ada, norm1_w, w_in_mix, conv_w, mlstm_gate_bias, mlstm_norm_w,
           w_conv_out, w_mlstm_out, w_o, norm2_w, w_gate_up, w_down, final_norm_w):
    assert x.shape == (BATCH, SEQ, D_MODEL) and w_ada.shape[0] == 1
    x2d = x.reshape(TOKENS, D_MODEL)

    ada = _ada_call(c, w_ada[0], b_ada[0])
    ada3 = ada.reshape(BATCH * 6, 1, D_MODEL)

    w_in = w_in_mix[0]
    w_all = lax.dynamic_update_slice(w_in[:, :IN_COLS - N_GATES].astype(BF16),
                                     w_in[:, OFF_BG:].astype(BF16), (0, OFF_G))
    w_gate = jnp.pad(w_in[:, OFF_G:OFF_BG], ((0, 0), (0, LANES - N_GATES))).astype(BF16)
    b_gate = jnp.pad(mlstm_gate_bias[0], (0, LANES - N_GATES)).reshape(1, LANES)

    main, kt, gpre = _inproj_call(x2d, ada3, norm1_w[0].reshape(1, D_MODEL),
                                  w_all, w_gate, b_gate)

    gates = gpre[:, :N_GATES].reshape(BATCH, N_CHUNKS, CHUNK, 4, M_HEADS)
    gates = gates.transpose(0, 4, 3, 1, 2)
    hm = _mlstm_call(gates, main.reshape(BATCH, SEQ, MAIN_COLS), kt,
                     mlstm_norm_w[0].reshape(1, M_V))

    x1 = _mix_call(x2d, main, hm.reshape(TOKENS, M_V), ada3, conv_w[0],
                   w_conv_out[0].astype(BF16), w_mlstm_out[0].astype(BF16), w_o[0].astype(BF16))

    act = _ffn_up_call(x1, ada3, norm2_w[0].reshape(1, D_MODEL), w_gate_up[0].astype(BF16))
    out = _ffn_down_call(act, x1, ada3, final_norm_w.reshape(1, D_MODEL), w_down[0].astype(BF16))
    return out.reshape(BATCH, SEQ, D_MODEL)
```

```python
import functools
import math

import jax
import jax.numpy as jnp
from jax import lax
from jax.experimental import pallas as pl
from jax.experimental.pallas import tpu as pltpu

D_MODEL = 2048
BATCH = 8
SEQ = 4096
TOKENS = BATCH * SEQ
EPS = 1e-6
CONV_WIDTH = D_MODEL // 2
M_HEADS = 8
QK_DIM = D_MODEL // 16
V_DIM = D_MODEL // 8
M_QK = M_HEADS * QK_DIM
M_V = M_HEADS * V_DIM
N_GATES = 4 * M_HEADS
FFN_HIDDEN = int(math.ceil((8 * D_MODEL / 3) / 256) * 256)

OFF_CB, OFF_CC, OFF_CX = 0, CONV_WIDTH, 2 * CONV_WIDTH
OFF_Q = 3 * CONV_WIDTH
OFF_K = OFF_Q + M_QK
OFF_V = OFF_K + M_QK
OFF_O = OFF_V + M_V
OFF_G = OFF_O + M_V
OFF_BG = OFF_G + N_GATES
IN_COLS = OFF_BG + 2 * D_MODEL

MAIN_COLS = IN_COLS - M_QK - N_GATES
MAIN_Q = 3 * CONV_WIDTH
MAIN_V = MAIN_Q + M_QK
MAIN_O = MAIN_V + M_V
MAIN_GC = MAIN_O + M_V
MAIN_GM = MAIN_GC + D_MODEL

LANES = 128
BF16_SUBLANES = 16
VMEM_LIMIT = 56 * 1024 * 1024

CHUNK = 256
N_CHUNKS = SEQ // CHUNK
MLSTM_UNROLL = 8
V_EXT = V_DIM + LANES

TM_IN = 1024
TN_IN = 1024
NJ_A = OFF_G // TN_IN
J_K0 = OFF_K // TN_IN
NJ_K = M_QK // TN_IN
NJ_BG = 2 * D_MODEL // TN_IN
assert NJ_K == 1 and OFF_G % TN_IN == 0
TM_MIX = 256
TM_UP = 1024
TM_DOWN = 256
TF_FFN = 512
NF_FFN = FFN_HIDDEN // TF_FFN
TN_ADA = 1024

F32 = jnp.float32
BF16 = jnp.bfloat16
LOG2E = math.log2(math.e)


def _dot(a, b):
    return jnp.dot(a, b, preferred_element_type=F32)


def _sigmoid(x):
    return 1.0 / (1.0 + jnp.exp(-x))


def _rms(x, norm_w):
    return x * lax.rsqrt(jnp.mean(x * x, axis=-1, keepdims=True) + EPS) * norm_w


def _ada_kernel(c_ref, w_ref, b_ref, o_ref):
    c = c_ref[...]
    c_act = (c * _sigmoid(c)).astype(BF16)
    o_ref[...] = _dot(c_act, w_ref[...].astype(BF16)) + b_ref[...]


def _ada_call(c, w_ada, b_ada):
    n = w_ada.shape[1]
    return pl.pallas_call(
        _ada_kernel,
        grid=(n // TN_ADA,),
        in_specs=[
            pl.BlockSpec((BATCH, D_MODEL), lambda j: (0, 0)),
            pl.BlockSpec((D_MODEL, TN_ADA), lambda j: (0, j)),
            pl.BlockSpec((1, TN_ADA), lambda j: (0, j)),
        ],
        out_specs=pl.BlockSpec((BATCH, TN_ADA), lambda j: (0, j)),
        out_shape=jax.ShapeDtypeStruct((BATCH, n), F32),
        compiler_params=pltpu.CompilerParams(
            dimension_semantics=("arbitrary",), vmem_limit_bytes=VMEM_LIMIT),
        name="ada",
    )(c, w_ada, b_ada.reshape(1, n))


def _next_tile_rows(step, n_rows, n_steps):
    rows = -(-n_rows // (n_steps * BF16_SUBLANES)) * BF16_SUBLANES
    start = jnp.minimum(step * rows, n_rows - rows)
    return pl.ds(pl.multiple_of(start, BF16_SUBLANES), rows)


def _inproj_kernel(x_ref, shift_ref, scale_ref, nw_ref, w_ref, wg_ref, bg_ref,
                   main_ref, kt_ref, g_ref, h_scr):
    r = pl.program_id(0)
    j = pl.program_id(1)
    cur = (r + 1) % 2
    nxt = r % 2
    is_k = (j >= J_K0) & (j < J_K0 + NJ_K)
    gain = nw_ref[...] * (1.0 + scale_ref[0])
    shift = shift_ref[0]

    def norm_next_rows():
        rows = _next_tile_rows(j, TM_IN, NJ_A + NJ_BG)
        h_scr[nxt, rows, :] = (_rms(x_ref[rows, :], gain) + shift).astype(BF16)

    @pl.when(r == 0)
    def _():
        norm_next_rows()

    @pl.when((r > 0) & (j == 0))
    def _():
        g_ref[...] = _dot(h_scr[cur], wg_ref[...]) + bg_ref[...]

    @pl.when((r > 0) & jnp.logical_not(is_k))
    def _():
        main_ref[...] = _dot(h_scr[cur], w_ref[...]).astype(BF16)
        norm_next_rows()

    @pl.when((r > 0) & is_k)
    def _():
        kt = (_dot(h_scr[cur], w_ref[...]) * (QK_DIM ** -0.5)).T
        for p in range(TM_IN // CHUNK):
            kt_ref[p] = kt[:, p * CHUNK:(p + 1) * CHUNK].astype(BF16)
        norm_next_rows()


def _inproj_call(x2d, ada3, norm_w, w_all, w_gate, b_gate):
    n_tiles = TOKENS // TM_IN
    tiles_per_seq = SEQ // TM_IN
    chunks_per_tile = TM_IN // CHUNK

    def next_tile(r):
        return jnp.minimum(r, n_tiles - 1)

    def this_tile(r):
        return jnp.maximum(r - 1, 0)

    def ada_map(k):
        return lambda r, j: ((next_tile(r) // tiles_per_seq) * 6 + k, 0, 0)

    def main_col(r, j):
        col = jnp.where(j < J_K0, j, jnp.maximum(j - NJ_K, J_K0 - 1))
        return jnp.where(r > 0, col, 0)

    return pl.pallas_call(
        _inproj_kernel,
        grid=(n_tiles + 1, NJ_A + NJ_BG),
        in_specs=[
            pl.BlockSpec((TM_IN, D_MODEL), lambda r, j: (next_tile(r), 0)),
            pl.BlockSpec((1, 1, D_MODEL), ada_map(0)),
            pl.BlockSpec((1, 1, D_MODEL), ada_map(1)),
            pl.BlockSpec((1, D_MODEL), lambda r, j: (0, 0)),
            pl.BlockSpec((D_MODEL, TN_IN), lambda r, j: (0, j)),
            pl.BlockSpec((D_MODEL, LANES), lambda r, j: (0, 0)),
            pl.BlockSpec((1, LANES), lambda r, j: (0, 0)),
        ],
        out_specs=[
            pl.BlockSpec((TM_IN, TN_IN), lambda r, j: (this_tile(r), main_col(r, j))),
            pl.BlockSpec((chunks_per_tile, M_QK, CHUNK), lambda r, j: (this_tile(r), 0, 0)),
            pl.BlockSpec((TM_IN, LANES), lambda r, j: (this_tile(r), 0)),
        ],
        out_shape=[
            jax.ShapeDtypeStruct((TOKENS, MAIN_COLS), BF16),
            jax.ShapeDtypeStruct((TOKENS // CHUNK, M_QK, CHUNK), BF16),
            jax.ShapeDtypeStruct((TOKENS, LANES), F32),
        ],
        scratch_shapes=[pltpu.VMEM((2, TM_IN, D_MODEL), BF16)],
        compiler_params=pltpu.CompilerParams(
            dimension_semantics=("arbitrary", "arbitrary"), vmem_limit_bytes=VMEM_LIMIT),
        name="inproj",
    )(x2d, ada3, ada3, norm_w, w_all, w_gate, b_gate)


def _log_sigmoid(x):
    return jnp.minimum(x, 0.0) - jnp.log1p(jnp.exp(-jnp.abs(x)))


def _mlstm_kernel(g_ref, q_ref, kt_ref, v_ref, o_ref, nw_ref, out_ref,
                  rows_scr, h_scr, cf_scr, cb_scr):
    L = CHUNK
    g = g_ref[0, 0]
    row_i = lax.broadcasted_iota(jnp.int32, (L, L), 0)
    col_i = lax.broadcasted_iota(jnp.int32, (L, L), 1)
    lower = col_i <= row_i
    upper = col_i >= row_i

    lf_f = _log_sigmoid(g[1])
    lf_b = _log_sigmoid(g[3])
    cum_f = jnp.dot(lf_f, upper.astype(F32), precision=lax.Precision.HIGHEST,
                    preferred_element_type=F32)
    cum_b = jnp.dot(lf_b, lower.astype(F32), precision=lax.Precision.HIGHEST,
                    preferred_element_type=F32)
    a_f = (g[0] - cum_f) * LOG2E
    a_b = (g[2] - cum_b) * LOG2E
    full = (N_CHUNKS, L)
    rows_scr[0] = lf_f * (-LOG2E)
    rows_scr[1] = a_f
    rows_scr[2] = jnp.broadcast_to(cum_f[:, L - 1:L] * LOG2E, full)
    rows_scr[3] = jnp.broadcast_to(jnp.max(a_f, axis=1, keepdims=True), full)
    rows_scr[4] = lf_b * (-LOG2E)
    rows_scr[5] = a_b
    rows_scr[6] = jnp.broadcast_to(cum_b[:, 0:1] * LOG2E, full)
    rows_scr[7] = jnp.broadcast_to(jnp.max(a_b, axis=1, keepdims=True), full)

    cf_scr[...] = jnp.zeros_like(cf_scr)
    cb_scr[...] = jnp.zeros_like(cb_scr)

    ones_block = jnp.ones((L, LANES), BF16)
    nw = nw_ref[...]

    def chunk(c, tri, base, c_scr, m_row):
        nlf_row = rows_scr[base, pl.ds(c, 1), :]
        a_row = rows_scr[base + 1, pl.ds(c, 1), :]
        tot_row = rows_scr[base + 2, pl.ds(c, 1), :]
        amax_row = rows_scr[base + 3, pl.ds(c, 1), :]
        r0 = pl.multiple_of(c * L, L)
        q = q_ref[0, pl.ds(r0, L), :]
        kt = kt_ref[c]
        v_ext = jnp.concatenate([v_ref[0, pl.ds(r0, L), :], ones_block], axis=1)

        ncum_col = jnp.sum(jnp.where(tri, nlf_row, 0.0), axis=1, keepdims=True)
        a_mat = jnp.where(tri, a_row, -jnp.inf)
        cm_col = jnp.max(a_mat, axis=1, keepdims=True)
        g_mat = jnp.maximum(cm_col, m_row)
        s = (_dot(q, kt) * jnp.exp2(a_mat - g_mat)).astype(BF16)
        g_blk = g_mat[:, :LANES]
        inter = jnp.exp2(m_row[:, :LANES] - g_blk)
        floor = jnp.exp2(ncum_col - g_blk)
        q_inter = q * inter.astype(BF16)
        c_state = c_scr[...]
        num = _dot(s, v_ext) + _dot(q_inter, c_state.astype(BF16))
        den = num[:, V_DIM:]
        r = 1.0 / jnp.maximum(jnp.abs(den), floor)
        h = num[:, :V_DIM] * jnp.tile(r, (1, V_DIM // LANES))

        g_row = jnp.maximum(m_row, amax_row)
        w = jnp.exp2(a_row - g_row)
        decay = jnp.exp2(m_row - g_row)
        ktw = kt * w.astype(BF16)
        c_scr[...] = jnp.tile(decay[:, :LANES], (1, V_EXT // LANES)) * c_state + _dot(ktw, v_ext)
        return h, tot_row + g_row

    def finish(c, h_sum):
        r0 = pl.multiple_of(c * L, L)
        hn = h_sum * lax.rsqrt(jnp.mean(h_sum * h_sum, axis=-1, keepdims=True) + EPS) * nw
        out_ref[0, pl.ds(r0, L), :] = _sigmoid(o_ref[0, pl.ds(r0, L), :]) * hn.astype(BF16)

    def step(i, carry, finalize):
        m_f, m_b = carry
        cf = i
        cb = N_CHUNKS - 1 - i
        h_f, m_f = chunk(cf, lower, 0, cf_scr, m_f)
        h_b, m_b = chunk(cb, upper, 4, cb_scr, m_b)
        rf = pl.multiple_of(cf * L, L)
        rb = pl.multiple_of(cb * L, L)
        if finalize:
            finish(cf, h_f + h_scr[pl.ds(rf, L), :])
            finish(cb, h_b + h_scr[pl.ds(rb, L), :])
        else:
            h_scr[pl.ds(rf, L), :] = h_f
            h_scr[pl.ds(rb, L), :] = h_b
        return m_f, m_b

    m0 = jnp.zeros((1, L), F32)
    half = N_CHUNKS // 2
    carry = lax.fori_loop(0, half, functools.partial(step, finalize=False), (m0, m0),
                          unroll=MLSTM_UNROLL)
    lax.fori_loop(half, N_CHUNKS, functools.partial(step, finalize=True), carry,
                  unroll=MLSTM_UNROLL)


def _mlstm_call(gates, main3, kt, norm_w_heads):
    q_blk = MAIN_Q // QK_DIM
    v_blk = MAIN_V // V_DIM
    o_blk = MAIN_O // V_DIM
    return pl.pallas_call(
        _mlstm_kernel,
        grid=(BATCH, M_HEADS),
        in_specs=[
            pl.BlockSpec((1, 1, 4, N_CHUNKS, CHUNK), lambda b, h: (b, h, 0, 0, 0)),
            pl.BlockSpec((1, SEQ, QK_DIM), lambda b, h: (b, 0, q_blk + h)),
            pl.BlockSpec((N_CHUNKS, QK_DIM, CHUNK), lambda b, h: (b, h, 0)),
            pl.BlockSpec((1, SEQ, V_DIM), lambda b, h: (b, 0, v_blk + h)),
            pl.BlockSpec((1, SEQ, V_DIM), lambda b, h: (b, 0, o_blk + h)),
            pl.BlockSpec((1, V_DIM), lambda b, h: (0, h)),
        ],
        out_specs=pl.BlockSpec((1, SEQ, V_DIM), lambda b, h: (b, 0, h)),
        out_shape=jax.ShapeDtypeStruct((BATCH, SEQ, M_V), BF16),
        scratch_shapes=[
            pltpu.VMEM((8, N_CHUNKS, CHUNK), F32),
            pltpu.VMEM((SEQ, V_DIM), F32),
            pltpu.VMEM((QK_DIM, V_EXT), F32),
            pltpu.VMEM((QK_DIM, V_EXT), F32),
        ],
        compiler_params=pltpu.CompilerParams(
            dimension_semantics=("arbitrary", "arbitrary"), vmem_limit_bytes=VMEM_LIMIT),
        name="mlstm",
    )(gates, main3, kt, main3, main3, norm_w_heads)


def _mix_kernel(x_ref, cb_ref, cc_ref, cx_ref, ccp_ref, cxp_ref, ccn_ref, cxn_ref,
                gc_ref, gm_ref, hm_ref, gate1_ref, convw_ref, wconv_ref, wml_ref, wo_ref,
                x1_ref):
    i = pl.program_id(0)
    tm = TM_MIX
    tiles_per_seq = SEQ // tm
    pos = i % tiles_per_seq

    u = cc_ref[...].astype(F32) * cx_ref[...].astype(F32)
    last = BF16_SUBLANES - 1
    u_prev = (ccp_ref[last:last + 1, :].astype(F32) * cxp_ref[last:last + 1, :].astype(F32))
    u_next = ccn_ref[0:1, :].astype(F32) * cxn_ref[0:1, :].astype(F32)
    u_prev = jnp.where(pos == 0, 0.0, u_prev)
    u_next = jnp.where(pos == tiles_per_seq - 1, 0.0, u_next)
    row = lax.broadcasted_iota(jnp.int32, (tm, 1), 0)
    u_m1 = jnp.where(row == 0, u_prev, pltpu.roll(u, 1, 0))
    u_p1 = jnp.where(row == tm - 1, u_next, pltpu.roll(u, tm - 1, 0))
    w = convw_ref[...]
    conv = w[0:1, :] * u_m1 + w[1:2, :] * u + w[2:3, :] * u_p1
    feat = (cb_ref[...].astype(F32) * conv).astype(BF16)
    y_conv = _dot(feat, wconv_ref[...])
    y_mlstm = _dot(hm_ref[...], wml_ref[...])
    merged = (_sigmoid(gc_ref[...].astype(F32)) * y_conv
              + _sigmoid(gm_ref[...].astype(F32)) * y_mlstm).astype(BF16)
    x1_ref[...] = x_ref[...] + gate1_ref[0] * _dot(merged, wo_ref[...])


def _mix_call(x2d, main, hm, ada3, conv_w, w_conv_out, w_mlstm_out, w_o):
    tm = TM_MIX
    tiles_per_seq = SEQ // tm
    halo = BF16_SUBLANES
    n_halo = TOKENS // halo
    per = tm // halo

    def col(k):
        return lambda i: (i, k)

    def prev_map(k):
        return lambda i: (jnp.maximum(i * per - 1, 0), k)

    def next_map(k):
        return lambda i: (jnp.minimum((i + 1) * per, n_halo - 1), k)

    def resident(shape):
        return pl.BlockSpec(shape, lambda i: (0, 0), pipeline_mode=pl.Buffered(1))

    cw = CONV_WIDTH
    return pl.pallas_call(
        _mix_kernel,
        grid=(TOKENS // tm,),
        in_specs=[
            pl.BlockSpec((tm, D_MODEL), lambda i: (i, 0)),
            pl.BlockSpec((tm, cw), col(0)),
            pl.BlockSpec((tm, cw), col(1)),
            pl.BlockSpec((tm, cw), col(2)),
            pl.BlockSpec((halo, cw), prev_map(1)),
            pl.BlockSpec((halo, cw), prev_map(2)),
            pl.BlockSpec((halo, cw), next_map(1)),
            pl.BlockSpec((halo, cw), next_map(2)),
            pl.BlockSpec((tm, D_MODEL), col(MAIN_GC // D_MODEL)),
            pl.BlockSpec((tm, D_MODEL), col(MAIN_GM // D_MODEL)),
            pl.BlockSpec((tm, M_V), lambda i: (i, 0)),
            pl.BlockSpec((1, 1, D_MODEL), lambda i: ((i // tiles_per_seq) * 6 + 2, 0, 0)),
            resident((3, cw)),
            resident((cw, D_MODEL)),
            resident((M_V, D_MODEL)),
            resident((D_MODEL, D_MODEL)),
        ],
        out_specs=pl.BlockSpec((tm, D_MODEL), lambda i: (i, 0)),
        out_shape=jax.ShapeDtypeStruct((TOKENS, D_MODEL), F32),
        compiler_params=pltpu.CompilerParams(
            dimension_semantics=("arbitrary",), vmem_limit_bytes=VMEM_LIMIT),
        name="mix",
    )(x2d, main, main, main, main, main, main, main, main, main, hm, ada3,
      conv_w, w_conv_out, w_mlstm_out, w_o)


def _ffn_up_kernel(x1_ref, shift_ref, scale_ref, nw_ref, wg_ref, wu_ref, act_ref, h_scr):
    r = pl.program_id(0)
    f = pl.program_id(1)
    cur = (r + 1) % 2
    nxt = r % 2
    gain = nw_ref[...] * (1.0 + scale_ref[0])
    shift = shift_ref[0]

    def norm_next_rows():
        rows = _next_tile_rows(f, TM_UP, NF_FFN)
        h_scr[nxt, rows, :] = (_rms(x1_ref[rows, :], gain) + shift).astype(BF16)

    @pl.when(r == 0)
    def _():
        norm_next_rows()

    @pl.when(r > 0)
    def _():
        h = h_scr[cur]
        gt = _dot(h, wg_ref[...])
        up = _dot(h, wu_ref[...])
        act_ref[...] = (gt * _sigmoid(gt) * up).astype(BF16)
        norm_next_rows()


def _ffn_up_call(x1, ada3, norm_w, w_gate_up):
    n_tiles = TOKENS // TM_UP
    tiles_per_seq = SEQ // TM_UP

    def next_tile(r):
        return jnp.minimum(r, n_tiles - 1)

    def ada_map(k):
        return lambda r, f: ((next_tile(r) // tiles_per_seq) * 6 + k, 0, 0)

    return pl.pallas_call(
        _ffn_up_kernel,
        grid=(n_tiles + 1, NF_FFN),
        in_specs=[
            pl.BlockSpec((TM_UP, D_MODEL), lambda r, f: (next_tile(r), 0)),
            pl.BlockSpec((1, 1, D_MODEL), ada_map(3)),
            pl.BlockSpec((1, 1, D_MODEL), ada_map(4)),
            pl.BlockSpec((1, D_MODEL), lambda r, f: (0, 0)),
            pl.BlockSpec((D_MODEL, TF_FFN), lambda r, f: (0, f)),
            pl.BlockSpec((D_MODEL, TF_FFN), lambda r, f: (0, NF_FFN + f)),
        ],
        out_specs=pl.BlockSpec((TM_UP, TF_FFN),
                               lambda r, f: (jnp.maximum(r - 1, 0), jnp.where(r > 0, f, 0))),
        out_shape=jax.ShapeDtypeStruct((TOKENS, FFN_HIDDEN), BF16),
        scratch_shapes=[pltpu.VMEM((2, TM_UP, D_MODEL), BF16)],
        compiler_params=pltpu.CompilerParams(
            dimension_semantics=("arbitrary", "arbitrary"), vmem_limit_bytes=VMEM_LIMIT),
        name="ffn_up",
    )(x1, ada3, ada3, norm_w, w_gate_up, w_gate_up)


def _ffn_down_kernel(act_ref, x1_ref, gate_ref, fnw_ref, wd_ref, out_ref):
    x2 = x1_ref[...] + gate_ref[0] * _dot(act_ref[...], wd_ref[...])
    out_ref[...] = _rms(x2, fnw_ref[...])


def _ffn_down_call(act, x1, ada3, final_norm_w, w_down):
    tm = TM_DOWN
    tiles_per_seq = SEQ // tm
    return pl.pallas_call(
        _ffn_down_kernel,
        grid=(TOKENS // tm,),
        in_specs=[
            pl.BlockSpec((tm, FFN_HIDDEN), lambda i: (i, 0)),
            pl.BlockSpec((tm, D_MODEL), lambda i: (i, 0)),
            pl.BlockSpec((1, 1, D_MODEL), lambda i: ((i // tiles_per_seq) * 6 + 5, 0, 0)),
            pl.BlockSpec((1, D_MODEL), lambda i: (0, 0)),
            pl.BlockSpec((FFN_HIDDEN, D_MODEL), lambda i: (0, 0), pipeline_mode=pl.Buffered(1)),
        ],
        out_specs=pl.BlockSpec((tm, D_MODEL), lambda i: (i, 0)),
        out_shape=jax.ShapeDtypeStruct((TOKENS, D_MODEL), F32),
        compiler_params=pltpu.CompilerParams(
            dimension_semantics=("arbitrary",), vmem_limit_bytes=VMEM_LIMIT),
        name="ffn_down",
    )(act, x1, ada3, final_norm_w, w_down)


def kernel(x, c, w_ada, b_ada, norm1_w, w_in_mix, conv_w, mlstm_gate_bias, mlstm_norm_w,
           w_conv_out, w_mlstm_out, w_o, norm2_w, w_gate_up, w_down, final_norm_w):
    assert x.shape == (BATCH, SEQ, D_MODEL) and w_ada.shape[0] == 1
    x2d = x.reshape(TOKENS, D_MODEL)

    ada = _ada_call(c, w_ada[0], b_ada[0])
    ada3 = ada.reshape(BATCH * 6, 1, D_MODEL)

    w_in = w_in_mix[0]
    w_all = lax.dynamic_update_slice(w_in.astype(BF16), w_in[:, OFF_BG:].astype(BF16), (0, OFF_G))
    w_gate = jnp.pad(w_in[:, OFF_G:OFF_BG], ((0, 0), (0, LANES - N_GATES))).astype(BF16)
    b_gate = jnp.pad(mlstm_gate_bias[0], (0, LANES - N_GATES)).reshape(1, LANES)

    main, kt, gpre = _inproj_call(x2d, ada3, norm1_w[0].reshape(1, D_MODEL),
                                  w_all, w_gate, b_gate)

    gates = gpre[:, :N_GATES].reshape(BATCH, N_CHUNKS, CHUNK, 4, M_HEADS)
    gates = gates.transpose(0, 4, 3, 1, 2)
    hm = _mlstm_call(gates, main.reshape(BATCH, SEQ, MAIN_COLS), kt,
                     mlstm_norm_w[0].reshape(1, M_V))

    x1 = _mix_call(x2d, main, hm.reshape(TOKENS, M_V), ada3, conv_w[0],
                   w_conv_out[0].astype(BF16), w_mlstm_out[0].astype(BF16), w_o[0].astype(BF16))

    act = _ffn_up_call(x1, ada3, norm2_w[0].reshape(1, D_MODEL), w_gate_up[0].astype(BF16))
    out = _ffn_down_call(act, x1, ada3, final_norm_w.reshape(1, D_MODEL), w_down[0].astype(BF16))
    return out.reshape(BATCH, SEQ, D_MODEL)
```

```python
import functools
import math

import jax
import jax.numpy as jnp
from jax import lax
from jax.experimental import pallas as pl
from jax.experimental.pallas import tpu as pltpu

D_MODEL = 2048
BATCH = 8
SEQ = 4096
TOKENS = BATCH * SEQ
EPS = 1e-6
CONV_WIDTH = D_MODEL // 2
M_HEADS = 8
QK_DIM = D_MODEL // 16
V_DIM = D_MODEL // 8
M_QK = M_HEADS * QK_DIM
M_V = M_HEADS * V_DIM
N_GATES = 4 * M_HEADS
FFN_HIDDEN = int(math.ceil((8 * D_MODEL / 3) / 256) * 256)

OFF_CB, OFF_CC, OFF_CX = 0, CONV_WIDTH, 2 * CONV_WIDTH
OFF_Q = 3 * CONV_WIDTH
OFF_K = OFF_Q + M_QK
OFF_V = OFF_K + M_QK
OFF_O = OFF_V + M_V
OFF_G = OFF_O + M_V
OFF_BG = OFF_G + N_GATES
IN_COLS = OFF_BG + 2 * D_MODEL

MAIN_COLS = IN_COLS - M_QK - N_GATES
MAIN_Q = 3 * CONV_WIDTH
MAIN_V = MAIN_Q + M_QK
MAIN_O = MAIN_V + M_V
MAIN_GC = MAIN_O + M_V
MAIN_GM = MAIN_GC + D_MODEL

LANES = 128
BF16_SUBLANES = 16
VMEM_LIMIT = 56 * 1024 * 1024

CHUNK = 256
N_CHUNKS = SEQ // CHUNK
MLSTM_UNROLL = 8
V_EXT = V_DIM + LANES

TM_IN = 1024
TN_IN = 1024
NJ_A = OFF_G // TN_IN
J_K0 = OFF_K // TN_IN
NJ_K = M_QK // TN_IN
NJ_BG = 2 * D_MODEL // TN_IN
assert NJ_K == 1 and OFF_G % TN_IN == 0
TM_MIX = 256
TM_UP = 1024
TM_DOWN = 256
TF_FFN = 512
NF_FFN = FFN_HIDDEN // TF_FFN
TN_ADA = 1024

F32 = jnp.float32
BF16 = jnp.bfloat16
LOG2E = math.log2(math.e)


def _dot(a, b):
    return jnp.dot(a, b, preferred_element_type=F32)


def _sigmoid(x):
    return 1.0 / (1.0 + jnp.exp(-x))


def _rms(x, norm_w):
    return x * lax.rsqrt(jnp.mean(x * x, axis=-1, keepdims=True) + EPS) * norm_w


def _ada_kernel(c_ref, w_ref, b_ref, o_ref):
    c = c_ref[...]
    c_act = (c * _sigmoid(c)).astype(BF16)
    o_ref[...] = _dot(c_act, w_ref[...].astype(BF16)) + b_ref[...]


def _ada_call(c, w_ada, b_ada):
    n = w_ada.shape[1]
    return pl.pallas_call(
        _ada_kernel,
        grid=(n // TN_ADA,),
        in_specs=[
            pl.BlockSpec((BATCH, D_MODEL), lambda j: (0, 0)),
            pl.BlockSpec((D_MODEL, TN_ADA), lambda j: (0, j)),
            pl.BlockSpec((1, TN_ADA), lambda j: (0, j)),
        ],
        out_specs=pl.BlockSpec((BATCH, TN_ADA), lambda j: (0, j)),
        out_shape=jax.ShapeDtypeStruct((BATCH, n), F32),
        compiler_params=pltpu.CompilerParams(
            dimension_semantics=("arbitrary",), vmem_limit_bytes=VMEM_LIMIT),
        name="ada",
    )(c, w_ada, b_ada.reshape(1, n))


def _next_tile_rows(step, n_rows, n_steps):
    rows = -(-n_rows // (n_steps * BF16_SUBLANES)) * BF16_SUBLANES
    start = jnp.minimum(step * rows, n_rows - rows)
    return pl.ds(pl.multiple_of(start, BF16_SUBLANES), rows)


def _inproj_kernel(x_ref, shift_ref, scale_ref, nw_ref, w_ref, wg_ref, bg_ref,
                   main_ref, kt_ref, g_ref, h_scr):
    r = pl.program_id(0)
    j = pl.program_id(1)
    cur = (r + 1) % 2
    nxt = r % 2
    is_k = (j >= J_K0) & (j < J_K0 + NJ_K)
    gain = nw_ref[...] * (1.0 + scale_ref[0])
    shift = shift_ref[0]

    def norm_next_rows():
        rows = _next_tile_rows(j, TM_IN, NJ_A + NJ_BG)
        h_scr[nxt, rows, :] = (_rms(x_ref[rows, :], gain) + shift).astype(BF16)

    @pl.when(r == 0)
    def _():
        norm_next_rows()

    @pl.when((r > 0) & (j == 0))
    def _():
        g_ref[...] = _dot(h_scr[cur], wg_ref[...]) + bg_ref[...]

    @pl.when((r > 0) & jnp.logical_not(is_k))
    def _():
        main_ref[...] = _dot(h_scr[cur], w_ref[...]).astype(BF16)
        norm_next_rows()

    @pl.when((r > 0) & is_k)
    def _():
        kt = (_dot(h_scr[cur], w_ref[...]) * (QK_DIM ** -0.5)).T
        for p in range(TM_IN // CHUNK):
            kt_ref[p] = kt[:, p * CHUNK:(p + 1) * CHUNK].astype(BF16)
        norm_next_rows()


def _inproj_call(x2d, ada3, norm_w, w_all, w_gate, b_gate):
    n_tiles = TOKENS // TM_IN
    tiles_per_seq = SEQ // TM_IN
    chunks_per_tile = TM_IN // CHUNK

    def next_tile(r):
        return jnp.minimum(r, n_tiles - 1)

    def this_tile(r):
        return jnp.maximum(r - 1, 0)

    def ada_map(k):
        return lambda r, j: ((next_tile(r) // tiles_per_seq) * 6 + k, 0, 0)

    def main_col(r, j):
        col = jnp.where(j < J_K0, j, jnp.maximum(j - NJ_K, J_K0 - 1))
        return jnp.where(r > 0, col, 0)

    return pl.pallas_call(
        _inproj_kernel,
        grid=(n_tiles + 1, NJ_A + NJ_BG),
        in_specs=[
            pl.BlockSpec((TM_IN, D_MODEL), lambda r, j: (next_tile(r), 0)),
            pl.BlockSpec((1, 1, D_MODEL), ada_map(0)),
            pl.BlockSpec((1, 1, D_MODEL), ada_map(1)),
            pl.BlockSpec((1, D_MODEL), lambda r, j: (0, 0)),
            pl.BlockSpec((D_MODEL, TN_IN), lambda r, j: (0, j)),
            pl.BlockSpec((D_MODEL, LANES), lambda r, j: (0, 0)),
            pl.BlockSpec((1, LANES), lambda r, j: (0, 0)),
        ],
        out_specs=[
            pl.BlockSpec((TM_IN, TN_IN), lambda r, j: (this_tile(r), main_col(r, j))),
            pl.BlockSpec((chunks_per_tile, M_QK, CHUNK), lambda r, j: (this_tile(r), 0, 0)),
            pl.BlockSpec((TM_IN, LANES), lambda r, j: (this_tile(r), 0)),
        ],
        out_shape=[
            jax.ShapeDtypeStruct((TOKENS, MAIN_COLS), BF16),
            jax.ShapeDtypeStruct((TOKENS // CHUNK, M_QK, CHUNK), BF16),
            jax.ShapeDtypeStruct((TOKENS, LANES), F32),
        ],
        scratch_shapes=[pltpu.VMEM((2, TM_IN, D_MODEL), BF16)],
        compiler_params=pltpu.CompilerParams(
            dimension_semantics=("arbitrary", "arbitrary"), vmem_limit_bytes=VMEM_LIMIT),
        name="inproj",
    )(x2d, ada3, ada3, norm_w, w_all, w_gate, b_gate)


def _log_sigmoid(x):
    return jnp.minimum(x, 0.0) - jnp.log1p(jnp.exp(-jnp.abs(x)))


def _mlstm_kernel(g_ref, q_ref, kt_ref, v_ref, o_ref, nw_ref, out_ref,
                  rows_scr, h_scr, cf_scr, cb_scr):
    L = CHUNK
    g = g_ref[0, 0]
    row_i = lax.broadcasted_iota(jnp.int32, (L, L), 0)
    col_i = lax.broadcasted_iota(jnp.int32, (L, L), 1)
    lower = col_i <= row_i
    upper = col_i >= row_i

    lf_f = _log_sigmoid(g[1])
    lf_b = _log_sigmoid(g[3])
    cum_f = jnp.dot(lf_f, upper.astype(F32), precision=lax.Precision.HIGHEST,
                    preferred_element_type=F32)
    cum_b = jnp.dot(lf_b, lower.astype(F32), precision=lax.Precision.HIGHEST,
                    preferred_element_type=F32)
    a_f = (g[0] - cum_f) * LOG2E
    a_b = (g[2] - cum_b) * LOG2E
    full = (N_CHUNKS, L)
    rows_scr[0] = lf_f * (-LOG2E)
    rows_scr[1] = a_f
    rows_scr[2] = jnp.broadcast_to(cum_f[:, L - 1:L] * LOG2E, full)
    rows_scr[3] = jnp.broadcast_to(jnp.max(a_f, axis=1, keepdims=True), full)
    rows_scr[4] = lf_b * (-LOG2E)
    rows_scr[5] = a_b
    rows_scr[6] = jnp.broadcast_to(cum_b[:, 0:1] * LOG2E, full)
    rows_scr[7] = jnp.broadcast_to(jnp.max(a_b, axis=1, keepdims=True), full)

    cf_scr[...] = jnp.zeros_like(cf_scr)
    cb_scr[...] = jnp.zeros_like(cb_scr)

    ones_block = jnp.ones((L, LANES), BF16)
    nw = nw_ref[...]

    def chunk(c, tri, base, c_scr, m_row):
        nlf_row = rows_scr[base, pl.ds(c, 1), :]
        a_row = rows_scr[base + 1, pl.ds(c, 1), :]
        tot_row = rows_scr[base + 2, pl.ds(c, 1), :]
        amax_row = rows_scr[base + 3, pl.ds(c, 1), :]
        r0 = pl.multiple_of(c * L, L)
        q = q_ref[0, pl.ds(r0, L), :]
        kt = kt_ref[c]
        v_ext = jnp.concatenate([v_ref[0, pl.ds(r0, L), :], ones_block], axis=1)

        ncum_col = jnp.sum(jnp.where(tri, nlf_row, 0.0), axis=1, keepdims=True)
        a_mat = jnp.where(tri, a_row, -jnp.inf)
        cm_col = jnp.max(a_mat, axis=1, keepdims=True)
        g_mat = jnp.maximum(cm_col, m_row)
        s = (_dot(q, kt) * jnp.exp2(a_mat - g_mat)).astype(BF16)
        g_blk = g_mat[:, :LANES]
        inter = jnp.exp2(m_row[:, :LANES] - g_blk)
        floor = jnp.exp2(ncum_col - g_blk)
        q_inter = q * inter.astype(BF16)
        c_state = c_scr[...]
        num = _dot(s, v_ext) + _dot(q_inter, c_state.astype(BF16))
        den = num[:, V_DIM:]
        r = 1.0 / jnp.maximum(jnp.abs(den), floor)
        h = num[:, :V_DIM] * jnp.tile(r, (1, V_DIM // LANES))

        g_row = jnp.maximum(m_row, amax_row)
        w = jnp.exp2(a_row - g_row)
        decay = jnp.exp2(m_row - g_row)
        ktw = kt * w.astype(BF16)
        c_scr[...] = jnp.tile(decay[:, :LANES], (1, V_EXT // LANES)) * c_state + _dot(ktw, v_ext)
        return h, tot_row + g_row

    def finish(c, h_sum):
        r0 = pl.multiple_of(c * L, L)
        hn = h_sum * lax.rsqrt(jnp.mean(h_sum * h_sum, axis=-1, keepdims=True) + EPS) * nw
        out_ref[0, pl.ds(r0, L), :] = _sigmoid(o_ref[0, pl.ds(r0, L), :]) * hn.astype(BF16)

    def step(i, carry, finalize):
        m_f, m_b = carry
        cf = i
        cb = N_CHUNKS - 1 - i
        h_f, m_f = chunk(cf, lower, 0, cf_scr, m_f)
        h_b, m_b = chunk(cb, upper, 4, cb_scr, m_b)
        rf = pl.multiple_of(cf * L, L)
        rb = pl.multiple_of(cb * L, L)
        if finalize:
            finish(cf, h_f + h_scr[pl.ds(rf, L), :])
            finish(cb, h_b + h_scr[pl.ds(rb, L), :])
        else:
            h_scr[pl.ds(rf, L), :] = h_f
            h_scr[pl.ds(rb, L), :] = h_b
        return m_f, m_b

    m0 = jnp.zeros((1, L), F32)
    half = N_CHUNKS // 2
    carry = lax.fori_loop(0, half, functools.partial(step, finalize=False), (m0, m0),
                          unroll=MLSTM_UNROLL)
    lax.fori_loop(half, N_CHUNKS, functools.partial(step, finalize=True), carry,
                  unroll=MLSTM_UNROLL)


def _mlstm_call(gates, main3, kt, norm_w_heads):
    q_blk = MAIN_Q // QK_DIM
    v_blk = MAIN_V // V_DIM
    o_blk = MAIN_O // V_DIM
    return pl.pallas_call(
        _mlstm_kernel,
        grid=(BATCH, M_HEADS),
        in_specs=[
            pl.BlockSpec((1, 1, 4, N_CHUNKS, CHUNK), lambda b, h: (b, h, 0, 0, 0)),
            pl.BlockSpec((1, SEQ, QK_DIM), lambda b, h: (b, 0, q_blk + h)),
            pl.BlockSpec((N_CHUNKS, QK_DIM, CHUNK), lambda b, h: (b, h, 0)),
            pl.BlockSpec((1, SEQ, V_DIM), lambda b, h: (b, 0, v_blk + h)),
            pl.BlockSpec((1, SEQ, V_DIM), lambda b, h: (b, 0, o_blk + h)),
            pl.BlockSpec((1, V_DIM), lambda b, h: (0, h)),
        ],
        out_specs=pl.BlockSpec((1, SEQ, V_DIM), lambda b, h: (b, 0, h)),
        out_shape=jax.ShapeDtypeStruct((BATCH, SEQ, M_V), BF16),
        scratch_shapes=[
            pltpu.VMEM((8, N_CHUNKS, CHUNK), F32),
            pltpu.VMEM((SEQ, V_DIM), F32),
            pltpu.VMEM((QK_DIM, V_EXT), F32),
            pltpu.VMEM((QK_DIM, V_EXT), F32),
        ],
        compiler_params=pltpu.CompilerParams(
            dimension_semantics=("arbitrary", "arbitrary"), vmem_limit_bytes=VMEM_LIMIT),
        name="mlstm",
    )(gates, main3, kt, main3, main3, norm_w_heads)


def _mix_kernel(x_ref, cb_ref, cc_ref, cx_ref, ccp_ref, cxp_ref, ccn_ref, cxn_ref,
                gc_ref, gm_ref, hm_ref, gate1_ref, convw_ref, wconv_ref, wml_ref, wo_ref,
                x1_ref, feat_scr):
    r = pl.program_id(0)
    tm = TM_MIX
    tiles_per_seq = SEQ // tm
    n_tiles = TOKENS // tm
    cur = (r + 1) % 2
    nxt = r % 2

    def conv_next_tile():
        pos = jnp.minimum(r, n_tiles - 1) % tiles_per_seq
        u = cc_ref[...].astype(F32) * cx_ref[...].astype(F32)
        last = BF16_SUBLANES - 1
        u_prev = ccp_ref[last:last + 1, :].astype(F32) * cxp_ref[last:last + 1, :].astype(F32)
        u_next = ccn_ref[0:1, :].astype(F32) * cxn_ref[0:1, :].astype(F32)
        u_prev = jnp.where(pos == 0, 0.0, u_prev)
        u_next = jnp.where(pos == tiles_per_seq - 1, 0.0, u_next)
        row = lax.broadcasted_iota(jnp.int32, (tm, 1), 0)
        u_m1 = jnp.where(row == 0, u_prev, pltpu.roll(u, 1, 0))
        u_p1 = jnp.where(row == tm - 1, u_next, pltpu.roll(u, tm - 1, 0))
        w = convw_ref[...]
        conv = w[0:1, :] * u_m1 + w[1:2, :] * u + w[2:3, :] * u_p1
        feat_scr[nxt] = (cb_ref[...].astype(F32) * conv).astype(BF16)

    @pl.when(r == 0)
    def _():
        conv_next_tile()

    @pl.when(r > 0)
    def _():
        y_conv = _dot(feat_scr[cur], wconv_ref[...])
        y_mlstm = _dot(hm_ref[...], wml_ref[...])
        merged = (_sigmoid(gc_ref[...].astype(F32)) * y_conv
                  + _sigmoid(gm_ref[...].astype(F32)) * y_mlstm).astype(BF16)
        x1_ref[...] = x_ref[...] + gate1_ref[0] * _dot(merged, wo_ref[...])
        conv_next_tile()


def _mix_call(x2d, main, hm, ada3, conv_w, w_conv_out, w_mlstm_out, w_o):
    tm = TM_MIX
    n_tiles = TOKENS // tm
    tiles_per_seq = SEQ // tm
    halo = BF16_SUBLANES
    n_halo = TOKENS // halo
    per = tm // halo

    def this_tile(r):
        return jnp.maximum(r - 1, 0)

    def next_tile(r):
        return jnp.minimum(r, n_tiles - 1)

    def col(k):
        return lambda r: (this_tile(r), k)

    def conv_col(k):
        return lambda r: (next_tile(r), k)

    def prev_map(k):
        return lambda r: (jnp.maximum(next_tile(r) * per - 1, 0), k)

    def next_map(k):
        return lambda r: (jnp.minimum((next_tile(r) + 1) * per, n_halo - 1), k)

    def resident(shape):
        return pl.BlockSpec(shape, lambda r: (0, 0), pipeline_mode=pl.Buffered(1))

    cw = CONV_WIDTH
    return pl.pallas_call(
        _mix_kernel,
        grid=(n_tiles + 1,),
        in_specs=[
            pl.BlockSpec((tm, D_MODEL), col(0)),
            pl.BlockSpec((tm, cw), conv_col(0)),
            pl.BlockSpec((tm, cw), conv_col(1)),
            pl.BlockSpec((tm, cw), conv_col(2)),
            pl.BlockSpec((halo, cw), prev_map(1)),
            pl.BlockSpec((halo, cw), prev_map(2)),
            pl.BlockSpec((halo, cw), next_map(1)),
            pl.BlockSpec((halo, cw), next_map(2)),
            pl.BlockSpec((tm, D_MODEL), col(MAIN_GC // D_MODEL)),
            pl.BlockSpec((tm, D_MODEL), col(MAIN_GM // D_MODEL)),
            pl.BlockSpec((tm, M_V), col(0)),
            pl.BlockSpec((1, 1, D_MODEL),
                         lambda r: ((this_tile(r) // tiles_per_seq) * 6 + 2, 0, 0)),
            resident((3, cw)),
            resident((cw, D_MODEL)),
            resident((M_V, D_MODEL)),
            resident((D_MODEL, D_MODEL)),
        ],
        out_specs=pl.BlockSpec((tm, D_MODEL), col(0)),
        out_shape=jax.ShapeDtypeStruct((TOKENS, D_MODEL), F32),
        scratch_shapes=[pltpu.VMEM((2, tm, CONV_WIDTH), BF16)],
        compiler_params=pltpu.CompilerParams(
            dimension_semantics=("arbitrary",), vmem_limit_bytes=VMEM_LIMIT),
        name="mix",
    )(x2d, main, main, main, main, main, main, main, main, main, hm, ada3,
      conv_w, w_conv_out, w_mlstm_out, w_o)


def _ffn_up_kernel(x1_ref, shift_ref, scale_ref, nw_ref, wg_ref, wu_ref, act_ref, h_scr):
    r = pl.program_id(0)
    f = pl.program_id(1)
    cur = (r + 1) % 2
    nxt = r % 2
    gain = nw_ref[...] * (1.0 + scale_ref[0])
    shift = shift_ref[0]

    def norm_next_rows():
        rows = _next_tile_rows(f, TM_UP, NF_FFN)
        h_scr[nxt, rows, :] = (_rms(x1_ref[rows, :], gain) + shift).astype(BF16)

    @pl.when(r == 0)
    def _():
        norm_next_rows()

    @pl.when(r > 0)
    def _():
        h = h_scr[cur]
        gt = _dot(h, wg_ref[...])
        up = _dot(h, wu_ref[...])
        act_ref[...] = (gt * _sigmoid(gt) * up).astype(BF16)
        norm_next_rows()


def _ffn_up_call(x1, ada3, norm_w, w_gate_up):
    n_tiles = TOKENS // TM_UP
    tiles_per_seq = SEQ // TM_UP

    def next_tile(r):
        return jnp.minimum(r, n_tiles - 1)

    def ada_map(k):
        return lambda r, f: ((next_tile(r) // tiles_per_seq) * 6 + k, 0, 0)

    return pl.pallas_call(
        _ffn_up_kernel,
        grid=(n_tiles + 1, NF_FFN),
        in_specs=[
            pl.BlockSpec((TM_UP, D_MODEL), lambda r, f: (next_tile(r), 0)),
            pl.BlockSpec((1, 1, D_MODEL), ada_map(3)),
            pl.BlockSpec((1, 1, D_MODEL), ada_map(4)),
            pl.BlockSpec((1, D_MODEL), lambda r, f: (0, 0)),
            pl.BlockSpec((D_MODEL, TF_FFN), lambda r, f: (0, f)),
            pl.BlockSpec((D_MODEL, TF_FFN), lambda r, f: (0, NF_FFN + f)),
        ],
        out_specs=pl.BlockSpec((TM_UP, TF_FFN),
                               lambda r, f: (jnp.maximum(r - 1, 0), jnp.where(r > 0, f, 0))),
        out_shape=jax.ShapeDtypeStruct((TOKENS, FFN_HIDDEN), BF16),
        scratch_shapes=[pltpu.VMEM((2, TM_UP, D_MODEL), BF16)],
        compiler_params=pltpu.CompilerParams(
            dimension_semantics=("arbitrary", "arbitrary"), vmem_limit_bytes=VMEM_LIMIT),
        name="ffn_up",
    )(x1, ada3, ada3, norm_w, w_gate_up, w_gate_up)


def _ffn_down_kernel(act_ref, x1_ref, gate_ref, fnw_ref, wd_ref, out_ref):
    x2 = x1_ref[...] + gate_ref[0] * _dot(act_ref[...], wd_ref[...])
    out_ref[...] = _rms(x2, fnw_ref[...])


def _ffn_down_call(act, x1, ada3, final_norm_w, w_down):
    tm = TM_DOWN
    tiles_per_seq = SEQ // tm
    return pl.pallas_call(
        _ffn_down_kernel,
        grid=(TOKENS // tm,),
        in_specs=[
            pl.BlockSpec((tm, FFN_HIDDEN), lambda i: (i, 0)),
            pl.BlockSpec((tm, D_MODEL), lambda i: (i, 0)),
            pl.BlockSpec((1, 1, D_MODEL), lambda i: ((i // tiles_per_seq) * 6 + 5, 0, 0)),
            pl.BlockSpec((1, D_MODEL), lambda i: (0, 0)),
            pl.BlockSpec((FFN_HIDDEN, D_MODEL), lambda i: (0, 0), pipeline_mode=pl.Buffered(1)),
        ],
        out_specs=pl.BlockSpec((tm, D_MODEL), lambda i: (i, 0)),
        out_shape=jax.ShapeDtypeStruct((TOKENS, D_MODEL), F32),
        compiler_params=pltpu.CompilerParams(
            dimension_semantics=("arbitrary",), vmem_limit_bytes=VMEM_LIMIT),
        name="ffn_down",
    )(act, x1, ada3, final_norm_w, w_down)


def kernel(x, c, w_ada, b_ada, norm1_w, w_in_mix, conv_w, mlstm_gate_bias, mlstm_norm_w,
           w_conv_out, w_mlstm_out, w_o, norm2_w, w_gate_up, w_down, final_norm_w):
    assert x.shape == (BATCH, SEQ, D_MODEL) and w_ada.shape[0] == 1
    x2d = x.reshape(TOKENS, D_MODEL)

    ada = _ada_call(c, w_ada[0], b_ada[0])
    ada3 = ada.reshape(BATCH * 6, 1, D_MODEL)

    w_in = w_in_mix[0]
    w_all = lax.dynamic_update_slice(w_in.astype(BF16), w_in[:, OFF_BG:].astype(BF16), (0, OFF_G))
    w_gate = jnp.pad(w_in[:, OFF_G:OFF_BG], ((0, 0), (0, LANES - N_GATES))).astype(BF16)
    b_gate = jnp.pad(mlstm_gate_bias[0], (0, LANES - N_GATES)).reshape(1, LANES)

    main, kt, gpre = _inproj_call(x2d, ada3, norm1_w[0].reshape(1, D_MODEL),
                                  w_all, w_gate, b_gate)

    gates = gpre[:, :N_GATES].reshape(BATCH, N_CHUNKS, CHUNK, 4, M_HEADS)
    gates = gates.transpose(0, 4, 3, 1, 2)
    hm = _mlstm_call(gates, main.reshape(BATCH, SEQ, MAIN_COLS), kt,
                     mlstm_norm_w[0].reshape(1, M_V))

    x1 = _mix_call(x2d, main, hm.reshape(TOKENS, M_V), ada3, conv_w[0],
                   w_conv_out[0].astype(BF16), w_mlstm_out[0].astype(BF16), w_o[0].astype(BF16))

    act = _ffn_up_call(x1, ada3, norm2_w[0].reshape(1, D_MODEL), w_gate_up[0].astype(BF16))
    out = _ffn_down_call(act, x1, ada3, final_norm_w.reshape(1, D_MODEL), w_down[0].astype(BF16))
    return out.reshape(BATCH, SEQ, D_MODEL)
```

```python
import functools
import math

import jax
import jax.numpy as jnp
from jax import lax
from jax.experimental import pallas as pl
from jax.experimental.pallas import tpu as pltpu

D_MODEL = 2048
BATCH = 8
SEQ = 4096
TOKENS = BATCH * SEQ
EPS = 1e-6
CONV_WIDTH = D_MODEL // 2
M_HEADS = 8
QK_DIM = D_MODEL // 16
V_DIM = D_MODEL // 8
M_QK = M_HEADS * QK_DIM
M_V = M_HEADS * V_DIM
N_GATES = 4 * M_HEADS
FFN_HIDDEN = int(math.ceil((8 * D_MODEL / 3) / 256) * 256)

OFF_CB, OFF_CC, OFF_CX = 0, CONV_WIDTH, 2 * CONV_WIDTH
OFF_Q = 3 * CONV_WIDTH
OFF_K = OFF_Q + M_QK
OFF_V = OFF_K + M_QK
OFF_O = OFF_V + M_V
OFF_G = OFF_O + M_V
OFF_BG = OFF_G + N_GATES
IN_COLS = OFF_BG + 2 * D_MODEL

MAIN_COLS = IN_COLS - M_QK - N_GATES
MAIN_Q = 3 * CONV_WIDTH
MAIN_V = MAIN_Q + M_QK
MAIN_O = MAIN_V + M_V
MAIN_GC = MAIN_O + M_V
MAIN_GM = MAIN_GC + D_MODEL

LANES = 128
BF16_SUBLANES = 16
VMEM_LIMIT = 56 * 1024 * 1024

CHUNK = 256
N_CHUNKS = SEQ // CHUNK
MLSTM_UNROLL = 8
V_EXT = V_DIM + LANES

TM_IN = 1024
TN_IN = 1024
NJ_A = OFF_G // TN_IN
J_K0 = OFF_K // TN_IN
NJ_K = M_QK // TN_IN
NJ_BG = 2 * D_MODEL // TN_IN
assert NJ_K == 1 and OFF_G % TN_IN == 0
TM_MIX = 256
TM_UP = 1024
TM_DOWN = 256
TF_FFN = 512
NF_FFN = FFN_HIDDEN // TF_FFN
TN_ADA = 1024

F32 = jnp.float32
BF16 = jnp.bfloat16
LOG2E = math.log2(math.e)


def _dot(a, b):
    return jnp.dot(a, b, preferred_element_type=F32)


def _sigmoid(x):
    return 1.0 / (1.0 + jnp.exp(-x))


def _rms(x, norm_w):
    return x * lax.rsqrt(jnp.mean(x * x, axis=-1, keepdims=True) + EPS) * norm_w


def _ada_kernel(c_ref, w_ref, b_ref, o_ref):
    c = c_ref[...]
    c_act = (c * _sigmoid(c)).astype(BF16)
    o_ref[...] = _dot(c_act, w_ref[...].astype(BF16)) + b_ref[...]


def _ada_call(c, w_ada, b_ada):
    n = w_ada.shape[1]
    return pl.pallas_call(
        _ada_kernel,
        grid=(n // TN_ADA,),
        in_specs=[
            pl.BlockSpec((BATCH, D_MODEL), lambda j: (0, 0)),
            pl.BlockSpec((D_MODEL, TN_ADA), lambda j: (0, j)),
            pl.BlockSpec((1, TN_ADA), lambda j: (0, j)),
        ],
        out_specs=pl.BlockSpec((BATCH, TN_ADA), lambda j: (0, j)),
        out_shape=jax.ShapeDtypeStruct((BATCH, n), F32),
        compiler_params=pltpu.CompilerParams(
            dimension_semantics=("arbitrary",), vmem_limit_bytes=VMEM_LIMIT),
        name="ada",
    )(c, w_ada, b_ada.reshape(1, n))


def _next_tile_rows(step, n_rows, n_steps):
    rows = -(-n_rows // (n_steps * BF16_SUBLANES)) * BF16_SUBLANES
    start = jnp.minimum(step * rows, n_rows - rows)
    return pl.ds(pl.multiple_of(start, BF16_SUBLANES), rows)


def _inproj_kernel(x_ref, shift_ref, scale_ref, nw_ref, w_ref, wg_ref, bg_ref,
                   wconv_ref, wml_ref, wo_ref,
                   main_ref, kt_ref, g_ref, wconv_out_ref, wml_out_ref, wo_out_ref, h_scr):
    wconv_out_ref[...] = wconv_ref[...].astype(BF16)
    wml_out_ref[...] = wml_ref[...].astype(BF16)
    wo_out_ref[...] = wo_ref[...].astype(BF16)
    r = pl.program_id(0)
    j = pl.program_id(1)
    cur = (r + 1) % 2
    nxt = r % 2
    is_k = (j >= J_K0) & (j < J_K0 + NJ_K)
    gain = nw_ref[...] * (1.0 + scale_ref[0])
    shift = shift_ref[0]

    def norm_next_rows():
        rows = _next_tile_rows(j, TM_IN, NJ_A + NJ_BG)
        h_scr[nxt, rows, :] = (_rms(x_ref[rows, :], gain) + shift).astype(BF16)

    @pl.when(r == 0)
    def _():
        norm_next_rows()

    @pl.when((r > 0) & (j == 0))
    def _():
        g_ref[...] = _dot(h_scr[cur], wg_ref[...]) + bg_ref[...]

    @pl.when((r > 0) & jnp.logical_not(is_k))
    def _():
        main_ref[...] = _dot(h_scr[cur], w_ref[...]).astype(BF16)
        norm_next_rows()

    @pl.when((r > 0) & is_k)
    def _():
        kt = (_dot(h_scr[cur], w_ref[...]) * (QK_DIM ** -0.5)).T
        for p in range(TM_IN // CHUNK):
            kt_ref[p] = kt[:, p * CHUNK:(p + 1) * CHUNK].astype(BF16)
        norm_next_rows()


def _inproj_call(x2d, ada3, norm_w, w_all, w_gate, b_gate, w_conv_out, w_mlstm_out, w_o):
    n_tiles = TOKENS // TM_IN
    assert CONV_WIDTH % (n_tiles * BF16_SUBLANES) == 0

    def slab(w):
        return pl.BlockSpec((w.shape[0] // n_tiles, w.shape[1]), lambda r, j: (next_tile(r), 0))

    def bf16_like(w):
        return jax.ShapeDtypeStruct(w.shape, BF16)

    tiles_per_seq = SEQ // TM_IN
    chunks_per_tile = TM_IN // CHUNK

    def next_tile(r):
        return jnp.minimum(r, n_tiles - 1)

    def this_tile(r):
        return jnp.maximum(r - 1, 0)

    def ada_map(k):
        return lambda r, j: ((next_tile(r) // tiles_per_seq) * 6 + k, 0, 0)

    def main_col(r, j):
        col = jnp.where(j < J_K0, j, jnp.maximum(j - NJ_K, J_K0 - 1))
        return jnp.where(r > 0, col, 0)

    return pl.pallas_call(
        _inproj_kernel,
        grid=(n_tiles + 1, NJ_A + NJ_BG),
        in_specs=[
            pl.BlockSpec((TM_IN, D_MODEL), lambda r, j: (next_tile(r), 0)),
            pl.BlockSpec((1, 1, D_MODEL), ada_map(0)),
            pl.BlockSpec((1, 1, D_MODEL), ada_map(1)),
            pl.BlockSpec((1, D_MODEL), lambda r, j: (0, 0)),
            pl.BlockSpec((D_MODEL, TN_IN), lambda r, j: (0, j)),
            pl.BlockSpec((D_MODEL, LANES), lambda r, j: (0, 0)),
            pl.BlockSpec((1, LANES), lambda r, j: (0, 0)),
            slab(w_conv_out),
            slab(w_mlstm_out),
            slab(w_o),
        ],
        out_specs=[
            pl.BlockSpec((TM_IN, TN_IN), lambda r, j: (this_tile(r), main_col(r, j))),
            pl.BlockSpec((chunks_per_tile, M_QK, CHUNK), lambda r, j: (this_tile(r), 0, 0)),
            pl.BlockSpec((TM_IN, LANES), lambda r, j: (this_tile(r), 0)),
            slab(w_conv_out),
            slab(w_mlstm_out),
            slab(w_o),
        ],
        out_shape=[
            jax.ShapeDtypeStruct((TOKENS, MAIN_COLS), BF16),
            jax.ShapeDtypeStruct((TOKENS // CHUNK, M_QK, CHUNK), BF16),
            jax.ShapeDtypeStruct((TOKENS, LANES), F32),
            bf16_like(w_conv_out),
            bf16_like(w_mlstm_out),
            bf16_like(w_o),
        ],
        scratch_shapes=[pltpu.VMEM((2, TM_IN, D_MODEL), BF16)],
        compiler_params=pltpu.CompilerParams(
            dimension_semantics=("arbitrary", "arbitrary"), vmem_limit_bytes=VMEM_LIMIT),
        name="inproj",
    )(x2d, ada3, ada3, norm_w, w_all, w_gate, b_gate, w_conv_out, w_mlstm_out, w_o)


def _log_sigmoid(x):
    return jnp.minimum(x, 0.0) - jnp.log1p(jnp.exp(-jnp.abs(x)))


def _mlstm_kernel(g_ref, q_ref, kt_ref, v_ref, o_ref, nw_ref, out_ref,
                  rows_scr, h_scr, cf_scr, cb_scr):
    L = CHUNK
    g = g_ref[0, 0]
    row_i = lax.broadcasted_iota(jnp.int32, (L, L), 0)
    col_i = lax.broadcasted_iota(jnp.int32, (L, L), 1)
    lower = col_i <= row_i
    upper = col_i >= row_i

    lf_f = _log_sigmoid(g[1])
    lf_b = _log_sigmoid(g[3])
    cum_f = jnp.dot(lf_f, upper.astype(F32), precision=lax.Precision.HIGHEST,
                    preferred_element_type=F32)
    cum_b = jnp.dot(lf_b, lower.astype(F32), precision=lax.Precision.HIGHEST,
                    preferred_element_type=F32)
    a_f = (g[0] - cum_f) * LOG2E
    a_b = (g[2] - cum_b) * LOG2E
    full = (N_CHUNKS, L)
    rows_scr[0] = lf_f * (-LOG2E)
    rows_scr[1] = a_f
    rows_scr[2] = jnp.broadcast_to(cum_f[:, L - 1:L] * LOG2E, full)
    rows_scr[3] = jnp.broadcast_to(jnp.max(a_f, axis=1, keepdims=True), full)
    rows_scr[4] = lf_b * (-LOG2E)
    rows_scr[5] = a_b
    rows_scr[6] = jnp.broadcast_to(cum_b[:, 0:1] * LOG2E, full)
    rows_scr[7] = jnp.broadcast_to(jnp.max(a_b, axis=1, keepdims=True), full)

    cf_scr[...] = jnp.zeros_like(cf_scr)
    cb_scr[...] = jnp.zeros_like(cb_scr)

    ones_block = jnp.ones((L, LANES), BF16)
    nw = nw_ref[...]

    def chunk(c, tri, base, c_scr, m_row):
        nlf_row = rows_scr[base, pl.ds(c, 1), :]
        a_row = rows_scr[base + 1, pl.ds(c, 1), :]
        tot_row = rows_scr[base + 2, pl.ds(c, 1), :]
        amax_row = rows_scr[base + 3, pl.ds(c, 1), :]
        r0 = pl.multiple_of(c * L, L)
        q = q_ref[0, pl.ds(r0, L), :]
        kt = kt_ref[c]
        v_ext = jnp.concatenate([v_ref[0, pl.ds(r0, L), :], ones_block], axis=1)

        ncum_col = jnp.sum(jnp.where(tri, nlf_row, 0.0), axis=1, keepdims=True)
        a_mat = jnp.where(tri, a_row, -jnp.inf)
        cm_col = jnp.max(a_mat, axis=1, keepdims=True)
        g_mat = jnp.maximum(cm_col, m_row)
        s = (_dot(q, kt) * jnp.exp2(a_mat - g_mat)).astype(BF16)
        g_blk = g_mat[:, :LANES]
        inter = jnp.exp2(m_row[:, :LANES] - g_blk)
        floor = jnp.exp2(ncum_col - g_blk)
        q_inter = q * inter.astype(BF16)
        c_state = c_scr[...]
        num = _dot(s, v_ext) + _dot(q_inter, c_state.astype(BF16))
        den = num[:, V_DIM:]
        r = 1.0 / jnp.maximum(jnp.abs(den), floor)
        h = num[:, :V_DIM] * jnp.tile(r, (1, V_DIM // LANES))

        g_row = jnp.maximum(m_row, amax_row)
        w = jnp.exp2(a_row - g_row)
        decay = jnp.exp2(m_row - g_row)
        ktw = kt * w.astype(BF16)
        c_scr[...] = jnp.tile(decay[:, :LANES], (1, V_EXT // LANES)) * c_state + _dot(ktw, v_ext)
        return h, tot_row + g_row

    def finish(c, h_sum):
        r0 = pl.multiple_of(c * L, L)
        hn = h_sum * lax.rsqrt(jnp.mean(h_sum * h_sum, axis=-1, keepdims=True) + EPS) * nw
        out_ref[0, pl.ds(r0, L), :] = _sigmoid(o_ref[0, pl.ds(r0, L), :]) * hn.astype(BF16)

    def step(i, carry, finalize):
        m_f, m_b = carry
        cf = i
        cb = N_CHUNKS - 1 - i
        h_f, m_f = chunk(cf, lower, 0, cf_scr, m_f)
        h_b, m_b = chunk(cb, upper, 4, cb_scr, m_b)
        rf = pl.multiple_of(cf * L, L)
        rb = pl.multiple_of(cb * L, L)
        if finalize:
            finish(cf, h_f + h_scr[pl.ds(rf, L), :])
            finish(cb, h_b + h_scr[pl.ds(rb, L), :])
        else:
            h_scr[pl.ds(rf, L), :] = h_f
            h_scr[pl.ds(rb, L), :] = h_b
        return m_f, m_b

    m0 = jnp.zeros((1, L), F32)
    half = N_CHUNKS // 2
    carry = lax.fori_loop(0, half, functools.partial(step, finalize=False), (m0, m0),
                          unroll=MLSTM_UNROLL)
    lax.fori_loop(half, N_CHUNKS, functools.partial(step, finalize=True), carry,
                  unroll=MLSTM_UNROLL)


def _mlstm_call(gates, main3, kt, norm_w_heads):
    q_blk = MAIN_Q // QK_DIM
    v_blk = MAIN_V // V_DIM
    o_blk = MAIN_O // V_DIM
    return pl.pallas_call(
        _mlstm_kernel,
        grid=(BATCH, M_HEADS),
        in_specs=[
            pl.BlockSpec((1, 1, 4, N_CHUNKS, CHUNK), lambda b, h: (b, h, 0, 0, 0)),
            pl.BlockSpec((1, SEQ, QK_DIM), lambda b, h: (b, 0, q_blk + h)),
            pl.BlockSpec((N_CHUNKS, QK_DIM, CHUNK), lambda b, h: (b, h, 0)),
            pl.BlockSpec((1, SEQ, V_DIM), lambda b, h: (b, 0, v_blk + h)),
            pl.BlockSpec((1, SEQ, V_DIM), lambda b, h: (b, 0, o_blk + h)),
            pl.BlockSpec((1, V_DIM), lambda b, h: (0, h)),
        ],
        out_specs=pl.BlockSpec((1, SEQ, V_DIM), lambda b, h: (b, 0, h)),
        out_shape=jax.ShapeDtypeStruct((BATCH, SEQ, M_V), BF16),
        scratch_shapes=[
            pltpu.VMEM((8, N_CHUNKS, CHUNK), F32),
            pltpu.VMEM((SEQ, V_DIM), F32),
            pltpu.VMEM((QK_DIM, V_EXT), F32),
            pltpu.VMEM((QK_DIM, V_EXT), F32),
        ],
        compiler_params=pltpu.CompilerParams(
            dimension_semantics=("arbitrary", "arbitrary"), vmem_limit_bytes=VMEM_LIMIT),
        name="mlstm",
    )(gates, main3, kt, main3, main3, norm_w_heads)


def _mix_kernel(x_ref, cb_ref, cc_ref, cx_ref, ccp_ref, cxp_ref, ccn_ref, cxn_ref,
                gc_ref, gm_ref, hm_ref, gate1_ref, convw_ref, wconv_ref, wml_ref, wo_ref,
                wgu_ref, wdn_ref, x1_ref, wgu_out_ref, wdn_out_ref, feat_scr):
    r = pl.program_id(0)
    wgu_out_ref[...] = wgu_ref[...].astype(BF16)
    wdn_out_ref[...] = wdn_ref[...].astype(BF16)
    tm = TM_MIX
    tiles_per_seq = SEQ // tm
    n_tiles = TOKENS // tm
    cur = (r + 1) % 2
    nxt = r % 2

    def conv_next_tile():
        pos = jnp.minimum(r, n_tiles - 1) % tiles_per_seq
        u = cc_ref[...].astype(F32) * cx_ref[...].astype(F32)
        last = BF16_SUBLANES - 1
        u_prev = ccp_ref[last:last + 1, :].astype(F32) * cxp_ref[last:last + 1, :].astype(F32)
        u_next = ccn_ref[0:1, :].astype(F32) * cxn_ref[0:1, :].astype(F32)
        u_prev = jnp.where(pos == 0, 0.0, u_prev)
        u_next = jnp.where(pos == tiles_per_seq - 1, 0.0, u_next)
        row = lax.broadcasted_iota(jnp.int32, (tm, 1), 0)
        u_m1 = jnp.where(row == 0, u_prev, pltpu.roll(u, 1, 0))
        u_p1 = jnp.where(row == tm - 1, u_next, pltpu.roll(u, tm - 1, 0))
        w = convw_ref[...]
        conv = w[0:1, :] * u_m1 + w[1:2, :] * u + w[2:3, :] * u_p1
        feat_scr[nxt] = (cb_ref[...].astype(F32) * conv).astype(BF16)

    @pl.when(r == 0)
    def _():
        conv_next_tile()

    @pl.when(r > 0)
    def _():
        y_conv = _dot(feat_scr[cur], wconv_ref[...])
        y_mlstm = _dot(hm_ref[...], wml_ref[...])
        merged = (_sigmoid(gc_ref[...].astype(F32)) * y_conv
                  + _sigmoid(gm_ref[...].astype(F32)) * y_mlstm).astype(BF16)
        x1_ref[...] = x_ref[...] + gate1_ref[0] * _dot(merged, wo_ref[...])
        conv_next_tile()


def _mix_call(x2d, main, hm, ada3, conv_w, w_conv_out, w_mlstm_out, w_o, w_gate_up, w_down):
    tm = TM_MIX
    n_tiles = TOKENS // tm
    gu_rows = D_MODEL // n_tiles
    dn_rows = 11 * BF16_SUBLANES
    dn_slabs = FFN_HIDDEN // dn_rows
    dn_steps = n_tiles // dn_slabs
    assert gu_rows % BF16_SUBLANES == 0 and FFN_HIDDEN % dn_rows == 0 and n_tiles % dn_slabs == 0
    tiles_per_seq = SEQ // tm
    halo = BF16_SUBLANES
    n_halo = TOKENS // halo
    per = tm // halo

    def this_tile(r):
        return jnp.maximum(r - 1, 0)

    def next_tile(r):
        return jnp.minimum(r, n_tiles - 1)

    def col(k):
        return lambda r: (this_tile(r), k)

    def conv_col(k):
        return lambda r: (next_tile(r), k)

    def prev_map(k):
        return lambda r: (jnp.maximum(next_tile(r) * per - 1, 0), k)

    def next_map(k):
        return lambda r: (jnp.minimum((next_tile(r) + 1) * per, n_halo - 1), k)

    def resident(shape):
        return pl.BlockSpec(shape, lambda r: (0, 0), pipeline_mode=pl.Buffered(1))

    def gu_map(r):
        return (next_tile(r), 0)

    def dn_map(r):
        return (next_tile(r) // dn_steps, 0)

    cw = CONV_WIDTH
    return pl.pallas_call(
        _mix_kernel,
        grid=(n_tiles + 1,),
        in_specs=[
            pl.BlockSpec((tm, D_MODEL), col(0)),
            pl.BlockSpec((tm, cw), conv_col(0)),
            pl.BlockSpec((tm, cw), conv_col(1)),
            pl.BlockSpec((tm, cw), conv_col(2)),
            pl.BlockSpec((halo, cw), prev_map(1)),
            pl.BlockSpec((halo, cw), prev_map(2)),
            pl.BlockSpec((halo, cw), next_map(1)),
            pl.BlockSpec((halo, cw), next_map(2)),
            pl.BlockSpec((tm, D_MODEL), col(MAIN_GC // D_MODEL)),
            pl.BlockSpec((tm, D_MODEL), col(MAIN_GM // D_MODEL)),
            pl.BlockSpec((tm, M_V), col(0)),
            pl.BlockSpec((1, 1, D_MODEL),
                         lambda r: ((this_tile(r) // tiles_per_seq) * 6 + 2, 0, 0)),
            resident((3, cw)),
            resident((cw, D_MODEL)),
            resident((M_V, D_MODEL)),
            resident((D_MODEL, D_MODEL)),
            pl.BlockSpec((gu_rows, 2 * FFN_HIDDEN), gu_map),
            pl.BlockSpec((dn_rows, D_MODEL), dn_map),
        ],
        out_specs=[
            pl.BlockSpec((tm, D_MODEL), col(0)),
            pl.BlockSpec((gu_rows, 2 * FFN_HIDDEN), gu_map),
            pl.BlockSpec((dn_rows, D_MODEL), dn_map),
        ],
        out_shape=[
            jax.ShapeDtypeStruct((TOKENS, D_MODEL), F32),
            jax.ShapeDtypeStruct((D_MODEL, 2 * FFN_HIDDEN), BF16),
            jax.ShapeDtypeStruct((FFN_HIDDEN, D_MODEL), BF16),
        ],
        scratch_shapes=[pltpu.VMEM((2, tm, CONV_WIDTH), BF16)],
        compiler_params=pltpu.CompilerParams(
            dimension_semantics=("arbitrary",), vmem_limit_bytes=VMEM_LIMIT),
        name="mix",
    )(x2d, main, main, main, main, main, main, main, main, main, hm, ada3,
      conv_w, w_conv_out, w_mlstm_out, w_o, w_gate_up, w_down)


def _ffn_up_kernel(x1_ref, shift_ref, scale_ref, nw_ref, wg_ref, wu_ref, act_ref, h_scr):
    r = pl.program_id(0)
    f = pl.program_id(1)
    cur = (r + 1) % 2
    nxt = r % 2
    gain = nw_ref[...] * (1.0 + scale_ref[0])
    shift = shift_ref[0]

    def norm_next_rows():
        rows = _next_tile_rows(f, TM_UP, NF_FFN)
        h_scr[nxt, rows, :] = (_rms(x1_ref[rows, :], gain) + shift).astype(BF16)

    @pl.when(r == 0)
    def _():
        norm_next_rows()

    @pl.when(r > 0)
    def _():
        h = h_scr[cur]
        gt = _dot(h, wg_ref[...])
        up = _dot(h, wu_ref[...])
        act_ref[...] = (gt * _sigmoid(gt) * up).astype(BF16)
        norm_next_rows()


def _ffn_up_call(x1, ada3, norm_w, w_gate_up):
    n_tiles = TOKENS // TM_UP
    tiles_per_seq = SEQ // TM_UP

    def next_tile(r):
        return jnp.minimum(r, n_tiles - 1)

    def ada_map(k):
        return lambda r, f: ((next_tile(r) // tiles_per_seq) * 6 + k, 0, 0)

    return pl.pallas_call(
        _ffn_up_kernel,
        grid=(n_tiles + 1, NF_FFN),
        in_specs=[
            pl.BlockSpec((TM_UP, D_MODEL), lambda r, f: (next_tile(r), 0)),
            pl.BlockSpec((1, 1, D_MODEL), ada_map(3)),
            pl.BlockSpec((1, 1, D_MODEL), ada_map(4)),
            pl.BlockSpec((1, D_MODEL), lambda r, f: (0, 0)),
            pl.BlockSpec((D_MODEL, TF_FFN), lambda r, f: (0, f)),
            pl.BlockSpec((D_MODEL, TF_FFN), lambda r, f: (0, NF_FFN + f)),
        ],
        out_specs=pl.BlockSpec((TM_UP, TF_FFN),
                               lambda r, f: (jnp.maximum(r - 1, 0), jnp.where(r > 0, f, 0))),
        out_shape=jax.ShapeDtypeStruct((TOKENS, FFN_HIDDEN), BF16),
        scratch_shapes=[pltpu.VMEM((2, TM_UP, D_MODEL), BF16)],
        compiler_params=pltpu.CompilerParams(
            dimension_semantics=("arbitrary", "arbitrary"), vmem_limit_bytes=VMEM_LIMIT),
        name="ffn_up",
    )(x1, ada3, ada3, norm_w, w_gate_up, w_gate_up)


def _ffn_down_kernel(act_ref, x1_ref, gate_ref, fnw_ref, wd_ref, out_ref):
    x2 = x1_ref[...] + gate_ref[0] * _dot(act_ref[...], wd_ref[...])
    out_ref[...] = _rms(x2, fnw_ref[...])


def _ffn_down_call(act, x1, ada3, final_norm_w, w_down):
    tm = TM_DOWN
    tiles_per_seq = SEQ // tm
    return pl.pallas_call(
        _ffn_down_kernel,
        grid=(TOKENS // tm,),
        in_specs=[
            pl.BlockSpec((tm, FFN_HIDDEN), lambda i: (i, 0)),
            pl.BlockSpec((tm, D_MODEL), lambda i: (i, 0)),
            pl.BlockSpec((1, 1, D_MODEL), lambda i: ((i // tiles_per_seq) * 6 + 5, 0, 0)),
            pl.BlockSpec((1, D_MODEL), lambda i: (0, 0)),
            pl.BlockSpec((FFN_HIDDEN, D_MODEL), lambda i: (0, 0), pipeline_mode=pl.Buffered(1)),
        ],
        out_specs=pl.BlockSpec((tm, D_MODEL), lambda i: (i, 0)),
        out_shape=jax.ShapeDtypeStruct((TOKENS, D_MODEL), F32),
        compiler_params=pltpu.CompilerParams(
            dimension_semantics=("arbitrary",), vmem_limit_bytes=VMEM_LIMIT),
        name="ffn_down",
    )(act, x1, ada3, final_norm_w, w_down)


def kernel(x, c, w_ada, b_ada, norm1_w, w_in_mix, conv_w, mlstm_gate_bias, mlstm_norm_w,
           w_conv_out, w_mlstm_out, w_o, norm2_w, w_gate_up, w_down, final_norm_w):
    assert x.shape == (BATCH, SEQ, D_MODEL) and w_ada.shape[0] == 1
    x2d = x.reshape(TOKENS, D_MODEL)

    ada = _ada_call(c, w_ada[0], b_ada[0])
    ada3 = ada.reshape(BATCH * 6, 1, D_MODEL)

    w_in = w_in_mix[0]
    w_all = lax.dynamic_update_slice(w_in.astype(BF16), w_in[:, OFF_BG:].astype(BF16), (0, OFF_G))
    w_gate = jnp.pad(w_in[:, OFF_G:OFF_BG], ((0, 0), (0, LANES - N_GATES))).astype(BF16)
    b_gate = jnp.pad(mlstm_gate_bias[0], (0, LANES - N_GATES)).reshape(1, LANES)

    main, kt, gpre, w_conv_bf16, w_ml_bf16, w_o_bf16 = _inproj_call(
        x2d, ada3, norm1_w[0].reshape(1, D_MODEL), w_all, w_gate, b_gate,
        w_conv_out[0], w_mlstm_out[0], w_o[0])

    gates = gpre[:, :N_GATES].reshape(BATCH, N_CHUNKS, CHUNK, 4, M_HEADS)
    gates = gates.transpose(0, 4, 3, 1, 2)
    hm = _mlstm_call(gates, main.reshape(BATCH, SEQ, MAIN_COLS), kt,
                     mlstm_norm_w[0].reshape(1, M_V))

    x1, w_gate_up_bf16, w_down_bf16 = _mix_call(
        x2d, main, hm.reshape(TOKENS, M_V), ada3, conv_w[0], w_conv_bf16, w_ml_bf16, w_o_bf16,
        w_gate_up[0], w_down[0])

    act = _ffn_up_call(x1, ada3, norm2_w[0].reshape(1, D_MODEL), w_gate_up_bf16)
    out = _ffn_down_call(act, x1, ada3, final_norm_w.reshape(1, D_MODEL), w_down_bf16)
    return out.reshape(BATCH, SEQ, D_MODEL)
```

```python
import functools
import math

import jax
import jax.numpy as jnp
from jax import lax
from jax.experimental import pallas as pl
from jax.experimental.pallas import tpu as pltpu

D_MODEL = 2048
BATCH = 8
SEQ = 4096
TOKENS = BATCH * SEQ
EPS = 1e-6
CONV_WIDTH = D_MODEL // 2
M_HEADS = 8
QK_DIM = D_MODEL // 16
V_DIM = D_MODEL // 8
M_QK = M_HEADS * QK_DIM
M_V = M_HEADS * V_DIM
N_GATES = 4 * M_HEADS
FFN_HIDDEN = int(math.ceil((8 * D_MODEL / 3) / 256) * 256)

OFF_CB, OFF_CC, OFF_CX = 0, CONV_WIDTH, 2 * CONV_WIDTH
OFF_Q = 3 * CONV_WIDTH
OFF_K = OFF_Q + M_QK
OFF_V = OFF_K + M_QK
OFF_O = OFF_V + M_V
OFF_G = OFF_O + M_V
OFF_BG = OFF_G + N_GATES
IN_COLS = OFF_BG + 2 * D_MODEL

MAIN_COLS = IN_COLS - M_QK - N_GATES
MAIN_Q = 3 * CONV_WIDTH
MAIN_V = MAIN_Q + M_QK
MAIN_O = MAIN_V + M_V
MAIN_GC = MAIN_O + M_V
MAIN_GM = MAIN_GC + D_MODEL

LANES = 128
BF16_SUBLANES = 16
VMEM_LIMIT = 56 * 1024 * 1024

CHUNK = 256
N_CHUNKS = SEQ // CHUNK
MLSTM_UNROLL = 8
V_EXT = V_DIM + LANES

TM_IN = 1024
TN_IN = 1024
NJ_A = OFF_G // TN_IN
J_K0 = OFF_K // TN_IN
NJ_K = M_QK // TN_IN
NJ_BG = 2 * D_MODEL // TN_IN
assert NJ_K == 1 and OFF_G % TN_IN == 0
TM_MIX = 256
TM_UP = 1024
TM_DOWN = 256
TF_FFN = 512
NF_FFN = FFN_HIDDEN // TF_FFN
TN_ADA = 1024

F32 = jnp.float32
BF16 = jnp.bfloat16
LOG2E = math.log2(math.e)


def _dot(a, b):
    return jnp.dot(a, b, preferred_element_type=F32)


def _sigmoid(x):
    return 1.0 / (1.0 + jnp.exp(-x))


def _rms(x, norm_w):
    return x * lax.rsqrt(jnp.mean(x * x, axis=-1, keepdims=True) + EPS) * norm_w


def _ada_kernel(c_ref, w_ref, b_ref, o_ref):
    c = c_ref[...]
    c_act = (c * _sigmoid(c)).astype(BF16)
    o_ref[...] = _dot(c_act, w_ref[...].astype(BF16)) + b_ref[...]


def _ada_call(c, w_ada, b_ada):
    n = w_ada.shape[1]
    return pl.pallas_call(
        _ada_kernel,
        grid=(n // TN_ADA,),
        in_specs=[
            pl.BlockSpec((BATCH, D_MODEL), lambda j: (0, 0)),
            pl.BlockSpec((D_MODEL, TN_ADA), lambda j: (0, j)),
            pl.BlockSpec((1, TN_ADA), lambda j: (0, j)),
        ],
        out_specs=pl.BlockSpec((BATCH, TN_ADA), lambda j: (0, j)),
        out_shape=jax.ShapeDtypeStruct((BATCH, n), F32),
        compiler_params=pltpu.CompilerParams(
            dimension_semantics=("arbitrary",), vmem_limit_bytes=VMEM_LIMIT),
        name="ada",
    )(c, w_ada, b_ada.reshape(1, n))


def _next_tile_rows(step, n_rows, n_steps):
    rows = -(-n_rows // (n_steps * BF16_SUBLANES)) * BF16_SUBLANES
    start = jnp.minimum(step * rows, n_rows - rows)
    return pl.ds(pl.multiple_of(start, BF16_SUBLANES), rows)


def _inproj_kernel(x_ref, shift_ref, scale_ref, nw_ref, w_ref, wg_ref, bg_ref,
                   main_ref, kt_ref, g_ref, h_scr):
    r = pl.program_id(0)
    j = pl.program_id(1)
    cur = (r + 1) % 2
    nxt = r % 2
    is_k = (j >= J_K0) & (j < J_K0 + NJ_K)
    gain = nw_ref[...] * (1.0 + scale_ref[0])
    shift = shift_ref[0]

    def norm_next_rows():
        rows = _next_tile_rows(j, TM_IN, NJ_A + NJ_BG)
        h_scr[nxt, rows, :] = (_rms(x_ref[rows, :], gain) + shift).astype(BF16)

    @pl.when(r == 0)
    def _():
        norm_next_rows()

    @pl.when((r > 0) & (j == 0))
    def _():
        g_ref[...] = _dot(h_scr[cur], wg_ref[...]) + bg_ref[...]

    @pl.when((r > 0) & jnp.logical_not(is_k))
    def _():
        main_ref[...] = _dot(h_scr[cur], w_ref[...]).astype(BF16)
        norm_next_rows()

    @pl.when((r > 0) & is_k)
    def _():
        kt = (_dot(h_scr[cur], w_ref[...]) * (QK_DIM ** -0.5)).T
        for p in range(TM_IN // CHUNK):
            kt_ref[p] = kt[:, p * CHUNK:(p + 1) * CHUNK].astype(BF16)
        norm_next_rows()


def _inproj_call(x2d, ada3, norm_w, w_all, w_gate, b_gate):
    n_tiles = TOKENS // TM_IN
    tiles_per_seq = SEQ // TM_IN
    chunks_per_tile = TM_IN // CHUNK

    def next_tile(r):
        return jnp.minimum(r, n_tiles - 1)

    def this_tile(r):
        return jnp.maximum(r - 1, 0)

    def ada_map(k):
        return lambda r, j: ((next_tile(r) // tiles_per_seq) * 6 + k, 0, 0)

    def main_col(r, j):
        col = jnp.where(j < J_K0, j, jnp.maximum(j - NJ_K, J_K0 - 1))
        return jnp.where(r > 0, col, 0)

    return pl.pallas_call(
        _inproj_kernel,
        grid=(n_tiles + 1, NJ_A + NJ_BG),
        in_specs=[
            pl.BlockSpec((TM_IN, D_MODEL), lambda r, j: (next_tile(r), 0)),
            pl.BlockSpec((1, 1, D_MODEL), ada_map(0)),
            pl.BlockSpec((1, 1, D_MODEL), ada_map(1)),
            pl.BlockSpec((1, D_MODEL), lambda r, j: (0, 0)),
            pl.BlockSpec((D_MODEL, TN_IN), lambda r, j: (0, j)),
            pl.BlockSpec((D_MODEL, LANES), lambda r, j: (0, 0)),
            pl.BlockSpec((1, LANES), lambda r, j: (0, 0)),
        ],
        out_specs=[
            pl.BlockSpec((TM_IN, TN_IN), lambda r, j: (this_tile(r), main_col(r, j))),
            pl.BlockSpec((chunks_per_tile, M_QK, CHUNK), lambda r, j: (this_tile(r), 0, 0)),
            pl.BlockSpec((TM_IN, LANES), lambda r, j: (this_tile(r), 0)),
        ],
        out_shape=[
            jax.ShapeDtypeStruct((TOKENS, MAIN_COLS), BF16),
            jax.ShapeDtypeStruct((TOKENS // CHUNK, M_QK, CHUNK), BF16),
            jax.ShapeDtypeStruct((TOKENS, LANES), F32),
        ],
        scratch_shapes=[pltpu.VMEM((2, TM_IN, D_MODEL), BF16)],
        compiler_params=pltpu.CompilerParams(
            dimension_semantics=("arbitrary", "arbitrary"), vmem_limit_bytes=VMEM_LIMIT),
        name="inproj",
    )(x2d, ada3, ada3, norm_w, w_all, w_gate, b_gate)


def _log_sigmoid(x):
    return jnp.minimum(x, 0.0) - jnp.log1p(jnp.exp(-jnp.abs(x)))


def _mlstm_kernel(g_ref, q_ref, kt_ref, v_ref, o_ref, nw_ref, out_ref,
                  rows_scr, h_scr, cf_scr, cb_scr):
    L = CHUNK
    g = g_ref[0, 0]
    row_i = lax.broadcasted_iota(jnp.int32, (L, L), 0)
    col_i = lax.broadcasted_iota(jnp.int32, (L, L), 1)
    lower = col_i <= row_i
    upper = col_i >= row_i

    lf_f = _log_sigmoid(g[1])
    lf_b = _log_sigmoid(g[3])
    cum_f = jnp.dot(lf_f, upper.astype(F32), precision=lax.Precision.HIGHEST,
                    preferred_element_type=F32)
    cum_b = jnp.dot(lf_b, lower.astype(F32), precision=lax.Precision.HIGHEST,
                    preferred_element_type=F32)
    a_f = (g[0] - cum_f) * LOG2E
    a_b = (g[2] - cum_b) * LOG2E
    full = (N_CHUNKS, L)
    rows_scr[0] = lf_f * (-LOG2E)
    rows_scr[1] = a_f
    rows_scr[2] = jnp.broadcast_to(cum_f[:, L - 1:L] * LOG2E, full)
    rows_scr[3] = jnp.broadcast_to(jnp.max(a_f, axis=1, keepdims=True), full)
    rows_scr[4] = lf_b * (-LOG2E)
    rows_scr[5] = a_b
    rows_scr[6] = jnp.broadcast_to(cum_b[:, 0:1] * LOG2E, full)
    rows_scr[7] = jnp.broadcast_to(jnp.max(a_b, axis=1, keepdims=True), full)

    cf_scr[...] = jnp.zeros_like(cf_scr)
    cb_scr[...] = jnp.zeros_like(cb_scr)

    ones_block = jnp.ones((L, LANES), BF16)
    nw = nw_ref[...]

    def chunk(c, tri, base, c_scr, m_row):
        nlf_row = rows_scr[base, pl.ds(c, 1), :]
        a_row = rows_scr[base + 1, pl.ds(c, 1), :]
        tot_row = rows_scr[base + 2, pl.ds(c, 1), :]
        amax_row = rows_scr[base + 3, pl.ds(c, 1), :]
        r0 = pl.multiple_of(c * L, L)
        q = q_ref[0, pl.ds(r0, L), :]
        kt = kt_ref[c]
        v_ext = jnp.concatenate([v_ref[0, pl.ds(r0, L), :], ones_block], axis=1)

        ncum_col = jnp.sum(jnp.where(tri, nlf_row, 0.0), axis=1, keepdims=True)
        a_mat = jnp.where(tri, a_row, -jnp.inf)
        cm_col = jnp.max(a_mat, axis=1, keepdims=True)
        g_mat = jnp.maximum(cm_col, m_row)
        s = (_dot(q, kt) * jnp.exp2(a_mat - g_mat)).astype(BF16)
        g_blk = g_mat[:, :LANES]
        inter = jnp.exp2(m_row[:, :LANES] - g_blk)
        floor = jnp.exp2(ncum_col - g_blk)
        q_inter = q * inter.astype(BF16)
        c_state = c_scr[...]
        num = _dot(s, v_ext) + _dot(q_inter, c_state.astype(BF16))
        den = num[:, V_DIM:]
        r = 1.0 / jnp.maximum(jnp.abs(den), floor)
        h = num[:, :V_DIM] * jnp.tile(r, (1, V_DIM // LANES))

        g_row = jnp.maximum(m_row, amax_row)
        w = jnp.exp2(a_row - g_row)
        decay = jnp.exp2(m_row - g_row)
        ktw = kt * w.astype(BF16)
        c_scr[...] = jnp.tile(decay[:, :LANES], (1, V_EXT // LANES)) * c_state + _dot(ktw, v_ext)
        return h, tot_row + g_row

    def finish(c, h_sum):
        r0 = pl.multiple_of(c * L, L)
        hn = h_sum * lax.rsqrt(jnp.mean(h_sum * h_sum, axis=-1, keepdims=True) + EPS) * nw
        out_ref[0, pl.ds(r0, L), :] = _sigmoid(o_ref[0, pl.ds(r0, L), :]) * hn.astype(BF16)

    def step(i, carry, finalize):
        m_f, m_b = carry
        cf = i
        cb = N_CHUNKS - 1 - i
        h_f, m_f = chunk(cf, lower, 0, cf_scr, m_f)
        h_b, m_b = chunk(cb, upper, 4, cb_scr, m_b)
        rf = pl.multiple_of(cf * L, L)
        rb = pl.multiple_of(cb * L, L)
        if finalize:
            finish(cf, h_f + h_scr[pl.ds(rf, L), :])
            finish(cb, h_b + h_scr[pl.ds(rb, L), :])
        else:
            h_scr[pl.ds(rf, L), :] = h_f
            h_scr[pl.ds(rb, L), :] = h_b
        return m_f, m_b

    m0 = jnp.zeros((1, L), F32)
    half = N_CHUNKS // 2
    carry = lax.fori_loop(0, half, functools.partial(step, finalize=False), (m0, m0),
                          unroll=MLSTM_UNROLL)
    lax.fori_loop(half, N_CHUNKS, functools.partial(step, finalize=True), carry,
                  unroll=MLSTM_UNROLL)


def _mlstm_call(gates, main3, kt, norm_w_heads):
    q_blk = MAIN_Q // QK_DIM
    v_blk = MAIN_V // V_DIM
    o_blk = MAIN_O // V_DIM
    return pl.pallas_call(
        _mlstm_kernel,
        grid=(BATCH, M_HEADS),
        in_specs=[
            pl.BlockSpec((1, 1, 4, N_CHUNKS, CHUNK), lambda b, h: (b, h, 0, 0, 0)),
            pl.BlockSpec((1, SEQ, QK_DIM), lambda b, h: (b, 0, q_blk + h)),
            pl.BlockSpec((N_CHUNKS, QK_DIM, CHUNK), lambda b, h: (b, h, 0)),
            pl.BlockSpec((1, SEQ, V_DIM), lambda b, h: (b, 0, v_blk + h)),
            pl.BlockSpec((1, SEQ, V_DIM), lambda b, h: (b, 0, o_blk + h)),
            pl.BlockSpec((1, V_DIM), lambda b, h: (0, h)),
        ],
        out_specs=pl.BlockSpec((1, SEQ, V_DIM), lambda b, h: (b, 0, h)),
        out_shape=jax.ShapeDtypeStruct((BATCH, SEQ, M_V), BF16),
        scratch_shapes=[
            pltpu.VMEM((8, N_CHUNKS, CHUNK), F32),
            pltpu.VMEM((SEQ, V_DIM), F32),
            pltpu.VMEM((QK_DIM, V_EXT), F32),
            pltpu.VMEM((QK_DIM, V_EXT), F32),
        ],
        compiler_params=pltpu.CompilerParams(
            dimension_semantics=("arbitrary", "arbitrary"), vmem_limit_bytes=VMEM_LIMIT),
        name="mlstm",
    )(gates, main3, kt, main3, main3, norm_w_heads)


def _mix_kernel(x_ref, cb_ref, cc_ref, cx_ref, ccp_ref, cxp_ref, ccn_ref, cxn_ref,
                gc_ref, gm_ref, hm_ref, gate1_ref, convw_ref, wconv_ref, wml_ref, wo_ref,
                wgu_ref, wdn_ref, x1_ref, wgu_out_ref, wdn_out_ref, feat_scr):
    r = pl.program_id(0)
    wgu_out_ref[...] = wgu_ref[...].astype(BF16)
    wdn_out_ref[...] = wdn_ref[...].astype(BF16)
    tm = TM_MIX
    tiles_per_seq = SEQ // tm
    n_tiles = TOKENS // tm
    cur = (r + 1) % 2
    nxt = r % 2

    def conv_next_tile():
        pos = jnp.minimum(r, n_tiles - 1) % tiles_per_seq
        u = cc_ref[...].astype(F32) * cx_ref[...].astype(F32)
        last = BF16_SUBLANES - 1
        u_prev = ccp_ref[last:last + 1, :].astype(F32) * cxp_ref[last:last + 1, :].astype(F32)
        u_next = ccn_ref[0:1, :].astype(F32) * cxn_ref[0:1, :].astype(F32)
        u_prev = jnp.where(pos == 0, 0.0, u_prev)
        u_next = jnp.where(pos == tiles_per_seq - 1, 0.0, u_next)
        row = lax.broadcasted_iota(jnp.int32, (tm, 1), 0)
        u_m1 = jnp.where(row == 0, u_prev, pltpu.roll(u, 1, 0))
        u_p1 = jnp.where(row == tm - 1, u_next, pltpu.roll(u, tm - 1, 0))
        w = convw_ref[...]
        conv = w[0:1, :] * u_m1 + w[1:2, :] * u + w[2:3, :] * u_p1
        feat_scr[nxt] = (cb_ref[...].astype(F32) * conv).astype(BF16)

    @pl.when(r == 0)
    def _():
        conv_next_tile()

    @pl.when(r > 0)
    def _():
        y_conv = _dot(feat_scr[cur], wconv_ref[...])
        y_mlstm = _dot(hm_ref[...], wml_ref[...])
        merged = (_sigmoid(gc_ref[...].astype(F32)) * y_conv
                  + _sigmoid(gm_ref[...].astype(F32)) * y_mlstm).astype(BF16)
        x1_ref[...] = x_ref[...] + gate1_ref[0] * _dot(merged, wo_ref[...])
        conv_next_tile()


def _mix_call(x2d, main, hm, ada3, conv_w, w_conv_out, w_mlstm_out, w_o, w_gate_up, w_down):
    tm = TM_MIX
    n_tiles = TOKENS // tm
    gu_rows = D_MODEL // n_tiles
    dn_slabs = max(s for s in range(1, n_tiles + 1)
                   if n_tiles % s == 0 and FFN_HIDDEN % (s * BF16_SUBLANES) == 0)
    dn_rows = FFN_HIDDEN // dn_slabs
    dn_steps = n_tiles // dn_slabs
    assert gu_rows % BF16_SUBLANES == 0
    tiles_per_seq = SEQ // tm
    halo = BF16_SUBLANES
    n_halo = TOKENS // halo
    per = tm // halo

    def this_tile(r):
        return jnp.maximum(r - 1, 0)

    def next_tile(r):
        return jnp.minimum(r, n_tiles - 1)

    def col(k):
        return lambda r: (this_tile(r), k)

    def conv_col(k):
        return lambda r: (next_tile(r), k)

    def prev_map(k):
        return lambda r: (jnp.maximum(next_tile(r) * per - 1, 0), k)

    def next_map(k):
        return lambda r: (jnp.minimum((next_tile(r) + 1) * per, n_halo - 1), k)

    def resident(shape):
        return pl.BlockSpec(shape, lambda r: (0, 0), pipeline_mode=pl.Buffered(1))

    def gu_map(r):
        return (next_tile(r), 0)

    def dn_map(r):
        return (next_tile(r) // dn_steps, 0)

    cw = CONV_WIDTH
    return pl.pallas_call(
        _mix_kernel,
        grid=(n_tiles + 1,),
        in_specs=[
            pl.BlockSpec((tm, D_MODEL), col(0)),
            pl.BlockSpec((tm, cw), conv_col(0)),
            pl.BlockSpec((tm, cw), conv_col(1)),
            pl.BlockSpec((tm, cw), conv_col(2)),
            pl.BlockSpec((halo, cw), prev_map(1)),
            pl.BlockSpec((halo, cw), prev_map(2)),
            pl.BlockSpec((halo, cw), next_map(1)),
            pl.BlockSpec((halo, cw), next_map(2)),
            pl.BlockSpec((tm, D_MODEL), col(MAIN_GC // D_MODEL)),
            pl.BlockSpec((tm, D_MODEL), col(MAIN_GM // D_MODEL)),
            pl.BlockSpec((tm, M_V), col(0)),
            pl.BlockSpec((1, 1, D_MODEL),
                         lambda r: ((this_tile(r) // tiles_per_seq) * 6 + 2, 0, 0)),
            resident((3, cw)),
            resident((cw, D_MODEL)),
            resident((M_V, D_MODEL)),
            resident((D_MODEL, D_MODEL)),
            pl.BlockSpec((gu_rows, 2 * FFN_HIDDEN), gu_map),
            pl.BlockSpec((dn_rows, D_MODEL), dn_map),
        ],
        out_specs=[
            pl.BlockSpec((tm, D_MODEL), col(0)),
            pl.BlockSpec((gu_rows, 2 * FFN_HIDDEN), gu_map),
            pl.BlockSpec((dn_rows, D_MODEL), dn_map),
        ],
        out_shape=[
            jax.ShapeDtypeStruct((TOKENS, D_MODEL), F32),
            jax.ShapeDtypeStruct((D_MODEL, 2 * FFN_HIDDEN), BF16),
            jax.ShapeDtypeStruct((FFN_HIDDEN, D_MODEL), BF16),
        ],
        scratch_shapes=[pltpu.VMEM((2, tm, CONV_WIDTH), BF16)],
        compiler_params=pltpu.CompilerParams(
            dimension_semantics=("arbitrary",), vmem_limit_bytes=VMEM_LIMIT),
        name="mix",
    )(x2d, main, main, main, main, main, main, main, main, main, hm, ada3,
      conv_w, w_conv_out, w_mlstm_out, w_o, w_gate_up, w_down)


def _ffn_up_kernel(x1_ref, shift_ref, scale_ref, nw_ref, wg_ref, wu_ref, act_ref, h_scr):
    r = pl.program_id(0)
    f = pl.program_id(1)
    cur = (r + 1) % 2
    nxt = r % 2
    gain = nw_ref[...] * (1.0 + scale_ref[0])
    shift = shift_ref[0]

    def norm_next_rows():
        rows = _next_tile_rows(f, TM_UP, NF_FFN)
        h_scr[nxt, rows, :] = (_rms(x1_ref[rows, :], gain) + shift).astype(BF16)

    @pl.when(r == 0)
    def _():
        norm_next_rows()

    @pl.when(r > 0)
    def _():
        h = h_scr[cur]
        gt = _dot(h, wg_ref[...])
        up = _dot(h, wu_ref[...])
        act_ref[...] = (gt * _sigmoid(gt) * up).astype(BF16)
        norm_next_rows()


def _ffn_up_call(x1, ada3, norm_w, w_gate_up):
    n_tiles = TOKENS // TM_UP
    tiles_per_seq = SEQ // TM_UP

    def next_tile(r):
        return jnp.minimum(r, n_tiles - 1)

    def ada_map(k):
        return lambda r, f: ((next_tile(r) // tiles_per_seq) * 6 + k, 0, 0)

    return pl.pallas_call(
        _ffn_up_kernel,
        grid=(n_tiles + 1, NF_FFN),
        in_specs=[
            pl.BlockSpec((TM_UP, D_MODEL), lambda r, f: (next_tile(r), 0)),
            pl.BlockSpec((1, 1, D_MODEL), ada_map(3)),
            pl.BlockSpec((1, 1, D_MODEL), ada_map(4)),
            pl.BlockSpec((1, D_MODEL), lambda r, f: (0, 0)),
            pl.BlockSpec((D_MODEL, TF_FFN), lambda r, f: (0, f)),
            pl.BlockSpec((D_MODEL, TF_FFN), lambda r, f: (0, NF_FFN + f)),
        ],
        out_specs=pl.BlockSpec((TM_UP, TF_FFN),
                               lambda r, f: (jnp.maximum(r - 1, 0), jnp.where(r > 0, f, 0))),
        out_shape=jax.ShapeDtypeStruct((TOKENS, FFN_HIDDEN), BF16),
        scratch_shapes=[pltpu.VMEM((2, TM_UP, D_MODEL), BF16)],
        compiler_params=pltpu.CompilerParams(
            dimension_semantics=("arbitrary", "arbitrary"), vmem_limit_bytes=VMEM_LIMIT),
        name="ffn_up",
    )(x1, ada3, ada3, norm_w, w_gate_up, w_gate_up)


def _ffn_down_kernel(act_ref, x1_ref, gate_ref, fnw_ref, wd_ref, out_ref):
    x2 = x1_ref[...] + gate_ref[0] * _dot(act_ref[...], wd_ref[...])
    out_ref[...] = _rms(x2, fnw_ref[...])


def _ffn_down_call(act, x1, ada3, final_norm_w, w_down):
    tm = TM_DOWN
    tiles_per_seq = SEQ // tm
    return pl.pallas_call(
        _ffn_down_kernel,
        grid=(TOKENS // tm,),
        in_specs=[
            pl.BlockSpec((tm, FFN_HIDDEN), lambda i: (i, 0)),
            pl.BlockSpec((tm, D_MODEL), lambda i: (i, 0)),
            pl.BlockSpec((1, 1, D_MODEL), lambda i: ((i // tiles_per_seq) * 6 + 5, 0, 0)),
            pl.BlockSpec((1, D_MODEL), lambda i: (0, 0)),
            pl.BlockSpec((FFN_HIDDEN, D_MODEL), lambda i: (0, 0), pipeline_mode=pl.Buffered(1)),
        ],
        out_specs=pl.BlockSpec((tm, D_MODEL), lambda i: (i, 0)),
        out_shape=jax.ShapeDtypeStruct((TOKENS, D_MODEL), F32),
        compiler_params=pltpu.CompilerParams(
            dimension_semantics=("arbitrary",), vmem_limit_bytes=VMEM_LIMIT),
        name="ffn_down",
    )(act, x1, ada3, final_norm_w, w_down)


def kernel(x, c, w_ada, b_ada, norm1_w, w_in_mix, conv_w, mlstm_gate_bias, mlstm_norm_w,
           w_conv_out, w_mlstm_out, w_o, norm2_w, w_gate_up, w_down, final_norm_w):
    assert x.shape == (BATCH, SEQ, D_MODEL) and w_ada.shape[0] == 1
    x2d = x.reshape(TOKENS, D_MODEL)

    ada = _ada_call(c, w_ada[0], b_ada[0])
    ada3 = ada.reshape(BATCH * 6, 1, D_MODEL)

    w_in = w_in_mix[0]
    w_all = lax.dynamic_update_slice(w_in.astype(BF16), w_in[:, OFF_BG:].astype(BF16), (0, OFF_G))
    w_gate = jnp.pad(w_in[:, OFF_G:OFF_BG], ((0, 0), (0, LANES - N_GATES))).astype(BF16)
    b_gate = jnp.pad(mlstm_gate_bias[0], (0, LANES - N_GATES)).reshape(1, LANES)

    main, kt, gpre = _inproj_call(x2d, ada3, norm1_w[0].reshape(1, D_MODEL),
                                  w_all, w_gate, b_gate)

    gates = gpre[:, :N_GATES].reshape(BATCH, N_CHUNKS, CHUNK, 4, M_HEADS)
    gates = gates.transpose(0, 4, 3, 1, 2)
    hm = _mlstm_call(gates, main.reshape(BATCH, SEQ, MAIN_COLS), kt,
                     mlstm_norm_w[0].reshape(1, M_V))

    x1, w_gate_up_bf16, w_down_bf16 = _mix_call(
        x2d, main, hm.reshape(TOKENS, M_V), ada3, conv_w[0], w_conv_out[0].astype(BF16),
        w_mlstm_out[0].astype(BF16), w_o[0].astype(BF16), w_gate_up[0], w_down[0])

    act = _ffn_up_call(x1, ada3, norm2_w[0].reshape(1, D_MODEL), w_gate_up_bf16)
    out = _ffn_down_call(act, x1, ada3, final_norm_w.reshape(1, D_MODEL), w_down_bf16)
    return out.reshape(BATCH, SEQ, D_MODEL)
```

```python
import math

import jax
import jax.numpy as jnp
from jax import lax
from jax.experimental import pallas as pl
from jax.experimental.pallas import tpu as pltpu

D_MODEL = 2048
BATCH = 8
SEQ = 4096
TOKENS = BATCH * SEQ
EPS = 1e-6
CONV_WIDTH = D_MODEL // 2
M_HEADS = 8
QK_DIM = D_MODEL // 16
V_DIM = D_MODEL // 8
M_QK = M_HEADS * QK_DIM
M_V = M_HEADS * V_DIM
N_GATES = 4 * M_HEADS
FFN_HIDDEN = int(math.ceil((8 * D_MODEL / 3) / 256) * 256)

OFF_CB, OFF_CC, OFF_CX = 0, CONV_WIDTH, 2 * CONV_WIDTH
OFF_Q = 3 * CONV_WIDTH
OFF_K = OFF_Q + M_QK
OFF_V = OFF_K + M_QK
OFF_O = OFF_V + M_V
OFF_G = OFF_O + M_V
OFF_BG = OFF_G + N_GATES
IN_COLS = OFF_BG + 2 * D_MODEL

MAIN_COLS = IN_COLS - M_QK - N_GATES
MAIN_Q = 3 * CONV_WIDTH
MAIN_V = MAIN_Q + M_QK
MAIN_O = MAIN_V + M_V
MAIN_GC = MAIN_O + M_V
MAIN_GM = MAIN_GC + D_MODEL

LANES = 128
BF16_SUBLANES = 16
VMEM_LIMIT = 56 * 1024 * 1024
MLSTM_VMEM_LIMIT = 60 * 1024 * 1024

CHUNK = 256
N_CHUNKS = SEQ // CHUNK
HEADS_PER_STEP = 2
V_EXT = V_DIM + LANES

TM_IN = 1024
TN_IN = 1024
NJ_A = OFF_G // TN_IN
J_K0 = OFF_K // TN_IN
NJ_K = M_QK // TN_IN
NJ_BG = 2 * D_MODEL // TN_IN
assert NJ_K == 1 and OFF_G % TN_IN == 0
TM_MIX = 256
TM_UP = 1024
TM_DOWN = 256
TF_FFN = 512
NF_FFN = FFN_HIDDEN // TF_FFN
TN_ADA = 1024

F32 = jnp.float32
BF16 = jnp.bfloat16
LOG2E = math.log2(math.e)


def _dot(a, b):
    return jnp.dot(a, b, preferred_element_type=F32)


def _sigmoid(x):
    return 1.0 / (1.0 + jnp.exp(-x))


def _rms(x, norm_w):
    return x * lax.rsqrt(jnp.mean(x * x, axis=-1, keepdims=True) + EPS) * norm_w


def _ada_kernel(c_ref, w_ref, b_ref, o_ref):
    c = c_ref[...]
    c_act = (c * _sigmoid(c)).astype(BF16)
    o_ref[...] = _dot(c_act, w_ref[...].astype(BF16)) + b_ref[...]


def _ada_call(c, w_ada, b_ada):
    n = w_ada.shape[1]
    return pl.pallas_call(
        _ada_kernel,
        grid=(n // TN_ADA,),
        in_specs=[
            pl.BlockSpec((BATCH, D_MODEL), lambda j: (0, 0)),
            pl.BlockSpec((D_MODEL, TN_ADA), lambda j: (0, j)),
            pl.BlockSpec((1, TN_ADA), lambda j: (0, j)),
        ],
        out_specs=pl.BlockSpec((BATCH, TN_ADA), lambda j: (0, j)),
        out_shape=jax.ShapeDtypeStruct((BATCH, n), F32),
        compiler_params=pltpu.CompilerParams(
            dimension_semantics=("arbitrary",), vmem_limit_bytes=VMEM_LIMIT),
        name="ada",
    )(c, w_ada, b_ada.reshape(1, n))


def _next_tile_rows(step, n_rows, n_steps):
    rows = -(-n_rows // (n_steps * BF16_SUBLANES)) * BF16_SUBLANES
    start = jnp.minimum(step * rows, n_rows - rows)
    return pl.ds(pl.multiple_of(start, BF16_SUBLANES), rows)


def _inproj_kernel(x_ref, shift_ref, scale_ref, nw_ref, w_ref, wg_ref, bg_ref,
                   main_ref, kt_ref, g_ref, h_scr):
    r = pl.program_id(0)
    j = pl.program_id(1)
    cur = (r + 1) % 2
    nxt = r % 2
    is_k = (j >= J_K0) & (j < J_K0 + NJ_K)
    gain = nw_ref[...] * (1.0 + scale_ref[0])
    shift = shift_ref[0]

    def norm_next_rows():
        rows = _next_tile_rows(j, TM_IN, NJ_A + NJ_BG)
        h_scr[nxt, rows, :] = (_rms(x_ref[rows, :], gain) + shift).astype(BF16)

    @pl.when(r == 0)
    def _():
        norm_next_rows()

    @pl.when((r > 0) & (j == 0))
    def _():
        g_ref[...] = _dot(h_scr[cur], wg_ref[...]) + bg_ref[...]

    @pl.when((r > 0) & jnp.logical_not(is_k))
    def _():
        main_ref[...] = _dot(h_scr[cur], w_ref[...]).astype(BF16)
        norm_next_rows()

    @pl.when((r > 0) & is_k)
    def _():
        kt = (_dot(h_scr[cur], w_ref[...]) * (QK_DIM ** -0.5)).T
        for p in range(TM_IN // CHUNK):
            kt_ref[p] = kt[:, p * CHUNK:(p + 1) * CHUNK].astype(BF16)
        norm_next_rows()


def _inproj_call(x2d, ada3, norm_w, w_all, w_gate, b_gate):
    n_tiles = TOKENS // TM_IN
    tiles_per_seq = SEQ // TM_IN
    chunks_per_tile = TM_IN // CHUNK

    def next_tile(r):
        return jnp.minimum(r, n_tiles - 1)

    def this_tile(r):
        return jnp.maximum(r - 1, 0)

    def ada_map(k):
        return lambda r, j: ((next_tile(r) // tiles_per_seq) * 6 + k, 0, 0)

    def main_col(r, j):
        col = jnp.where(j < J_K0, j, jnp.maximum(j - NJ_K, J_K0 - 1))
        return jnp.where(r > 0, col, 0)

    return pl.pallas_call(
        _inproj_kernel,
        grid=(n_tiles + 1, NJ_A + NJ_BG),
        in_specs=[
            pl.BlockSpec((TM_IN, D_MODEL), lambda r, j: (next_tile(r), 0)),
            pl.BlockSpec((1, 1, D_MODEL), ada_map(0)),
            pl.BlockSpec((1, 1, D_MODEL), ada_map(1)),
            pl.BlockSpec((1, D_MODEL), lambda r, j: (0, 0)),
            pl.BlockSpec((D_MODEL, TN_IN), lambda r, j: (0, j)),
            pl.BlockSpec((D_MODEL, LANES), lambda r, j: (0, 0)),
            pl.BlockSpec((1, LANES), lambda r, j: (0, 0)),
        ],
        out_specs=[
            pl.BlockSpec((TM_IN, TN_IN), lambda r, j: (this_tile(r), main_col(r, j))),
            pl.BlockSpec((chunks_per_tile, M_QK, CHUNK), lambda r, j: (this_tile(r), 0, 0)),
            pl.BlockSpec((TM_IN, LANES), lambda r, j: (this_tile(r), 0)),
        ],
        out_shape=[
            jax.ShapeDtypeStruct((TOKENS, MAIN_COLS), BF16),
            jax.ShapeDtypeStruct((TOKENS // CHUNK, M_QK, CHUNK), BF16),
            jax.ShapeDtypeStruct((TOKENS, LANES), F32),
        ],
        scratch_shapes=[pltpu.VMEM((2, TM_IN, D_MODEL), BF16)],
        compiler_params=pltpu.CompilerParams(
            dimension_semantics=("arbitrary", "arbitrary"), vmem_limit_bytes=VMEM_LIMIT),
        name="inproj",
    )(x2d, ada3, ada3, norm_w, w_all, w_gate, b_gate)


def _log_sigmoid(x):
    return jnp.minimum(x, 0.0) - jnp.log1p(jnp.exp(-jnp.abs(x)))


def _mlstm_kernel(g_ref, q_ref, kt_ref, v_ref, o_ref, nw_ref, out_ref,
                  rows_scr, h_scr, cf_scr, cb_scr):
    L = CHUNK
    row_i = lax.broadcasted_iota(jnp.int32, (L, L), 0)
    col_i = lax.broadcasted_iota(jnp.int32, (L, L), 1)
    lower = col_i <= row_i
    upper = col_i >= row_i
    full = (N_CHUNKS, L)

    for hh in range(HEADS_PER_STEP):
        g = g_ref[0, hh]
        lf_f = _log_sigmoid(g[1])
        lf_b = _log_sigmoid(g[3])
        cum_f = jnp.dot(lf_f, upper.astype(F32), precision=lax.Precision.HIGHEST,
                        preferred_element_type=F32)
        cum_b = jnp.dot(lf_b, lower.astype(F32), precision=lax.Precision.HIGHEST,
                        preferred_element_type=F32)
        a_f = (g[0] - cum_f) * LOG2E
        a_b = (g[2] - cum_b) * LOG2E
        rows_scr[hh, 0] = lf_f * (-LOG2E)
        rows_scr[hh, 1] = a_f
        rows_scr[hh, 2] = jnp.broadcast_to(cum_f[:, L - 1:L] * LOG2E, full)
        rows_scr[hh, 3] = jnp.broadcast_to(jnp.max(a_f, axis=1, keepdims=True), full)
        rows_scr[hh, 4] = lf_b * (-LOG2E)
        rows_scr[hh, 5] = a_b
        rows_scr[hh, 6] = jnp.broadcast_to(cum_b[:, 0:1] * LOG2E, full)
        rows_scr[hh, 7] = jnp.broadcast_to(jnp.max(a_b, axis=1, keepdims=True), full)

    cf_scr[...] = jnp.zeros_like(cf_scr)
    cb_scr[...] = jnp.zeros_like(cb_scr)

    ones_block = jnp.ones((L, LANES), BF16)

    def chunk(hh, c, tri, base, c_scr, m_row):
        nlf_row = rows_scr[hh, base, pl.ds(c, 1), :]
        a_row = rows_scr[hh, base + 1, pl.ds(c, 1), :]
        tot_row = rows_scr[hh, base + 2, pl.ds(c, 1), :]
        amax_row = rows_scr[hh, base + 3, pl.ds(c, 1), :]
        rows = pl.ds(c * L, L)
        q = q_ref[0, rows, hh * QK_DIM:(hh + 1) * QK_DIM]
        kt = kt_ref[c, hh * QK_DIM:(hh + 1) * QK_DIM, :]
        v_ext = jnp.concatenate([v_ref[0, rows, hh * V_DIM:(hh + 1) * V_DIM], ones_block],
                                axis=1)

        ncum_col = jnp.sum(jnp.where(tri, nlf_row, 0.0), axis=1, keepdims=True)
        a_mat = jnp.where(tri, a_row, -jnp.inf)
        cm_col = jnp.max(a_mat, axis=1, keepdims=True)
        g_mat = jnp.maximum(cm_col, m_row)
        s = (_dot(q, kt) * jnp.exp2(a_mat - g_mat)).astype(BF16)
        g_blk = g_mat[:, :LANES]
        inter = jnp.exp2(m_row[:, :LANES] - g_blk)
        floor = jnp.exp2(ncum_col - g_blk)
        q_inter = q * inter.astype(BF16)
        c_state = c_scr[hh]
        num = _dot(s, v_ext) + _dot(q_inter, c_state.astype(BF16))
        den = num[:, V_DIM:]
        r = 1.0 / jnp.maximum(jnp.abs(den), floor)
        h = num[:, :V_DIM] * jnp.tile(r, (1, V_DIM // LANES))

        g_row = jnp.maximum(m_row, amax_row)
        w = jnp.exp2(a_row - g_row)
        decay = jnp.exp2(m_row - g_row)
        ktw = kt * w.astype(BF16)
        c_scr[hh] = jnp.tile(decay[:, :LANES], (1, V_EXT // LANES)) * c_state + _dot(ktw, v_ext)
        return h, tot_row + g_row

    def finish(hh, c, h_sum):
        rows = pl.ds(c * L, L)
        cols = slice(hh * V_DIM, (hh + 1) * V_DIM)
        hn = (h_sum * lax.rsqrt(jnp.mean(h_sum * h_sum, axis=-1, keepdims=True) + EPS)
              * nw_ref[:, cols])
        out_ref[0, rows, cols] = _sigmoid(o_ref[0, rows, cols]) * hn.astype(BF16)

    m_f = [jnp.zeros((1, L), F32)] * HEADS_PER_STEP
    m_b = [jnp.zeros((1, L), F32)] * HEADS_PER_STEP
    for i in range(N_CHUNKS):
        cf, cb = i, N_CHUNKS - 1 - i
        for hh in range(HEADS_PER_STEP):
            cols = slice(hh * V_DIM, (hh + 1) * V_DIM)
            h_f, m_f[hh] = chunk(hh, cf, lower, 0, cf_scr, m_f[hh])
            h_b, m_b[hh] = chunk(hh, cb, upper, 4, cb_scr, m_b[hh])
            if i >= N_CHUNKS // 2:
                finish(hh, cf, h_f + h_scr[cf * L:(cf + 1) * L, cols])
                finish(hh, cb, h_b + h_scr[cb * L:(cb + 1) * L, cols])
            else:
                h_scr[cf * L:(cf + 1) * L, cols] = h_f
                h_scr[cb * L:(cb + 1) * L, cols] = h_b


def _mlstm_call(gates, main3, kt, norm_w_heads):
    hp = HEADS_PER_STEP
    q_blk = MAIN_Q // (hp * QK_DIM)
    v_blk = MAIN_V // (hp * V_DIM)
    o_blk = MAIN_O // (hp * V_DIM)
    return pl.pallas_call(
        _mlstm_kernel,
        grid=(BATCH, M_HEADS // hp),
        in_specs=[
            pl.BlockSpec((1, hp, 4, N_CHUNKS, CHUNK), lambda b, h: (b, h, 0, 0, 0)),
            pl.BlockSpec((1, SEQ, hp * QK_DIM), lambda b, h: (b, 0, q_blk + h)),
            pl.BlockSpec((N_CHUNKS, hp * QK_DIM, CHUNK), lambda b, h: (b, h, 0)),
            pl.BlockSpec((1, SEQ, hp * V_DIM), lambda b, h: (b, 0, v_blk + h)),
            pl.BlockSpec((1, SEQ, hp * V_DIM), lambda b, h: (b, 0, o_blk + h)),
            pl.BlockSpec((1, hp * V_DIM), lambda b, h: (0, h)),
        ],
        out_specs=pl.BlockSpec((1, SEQ, hp * V_DIM), lambda b, h: (b, 0, h)),
        out_shape=jax.ShapeDtypeStruct((BATCH, SEQ, M_V), BF16),
        scratch_shapes=[
            pltpu.VMEM((hp, 8, N_CHUNKS, CHUNK), F32),
            pltpu.VMEM((SEQ, hp * V_DIM), F32),
            pltpu.VMEM((hp, QK_DIM, V_EXT), F32),
            pltpu.VMEM((hp, QK_DIM, V_EXT), F32),
        ],
        compiler_params=pltpu.CompilerParams(
            dimension_semantics=("arbitrary", "arbitrary"), vmem_limit_bytes=MLSTM_VMEM_LIMIT),
        name="mlstm",
    )(gates, main3, kt, main3, main3, norm_w_heads)


def _mix_kernel(x_ref, cb_ref, cc_ref, cx_ref, ccp_ref, cxp_ref, ccn_ref, cxn_ref,
                gc_ref, gm_ref, hm_ref, gate1_ref, convw_ref, wconv_ref, wml_ref, wo_ref,
                wgu_ref, wdn_ref, x1_ref, wgu_out_ref, wdn_out_ref, feat_scr):
    r = pl.program_id(0)
    tm = TM_MIX
    tiles_per_seq = SEQ // tm
    n_tiles = TOKENS // tm
    cur = (r + 1) % 2
    nxt = r % 2

    def conv_next_tile():
        pos = jnp.minimum(r, n_tiles - 1) % tiles_per_seq
        u = cc_ref[...].astype(F32) * cx_ref[...].astype(F32)
        last = BF16_SUBLANES - 1
        u_prev = ccp_ref[last:last + 1, :].astype(F32) * cxp_ref[last:last + 1, :].astype(F32)
        u_next = ccn_ref[0:1, :].astype(F32) * cxn_ref[0:1, :].astype(F32)
        u_prev = jnp.where(pos == 0, 0.0, u_prev)
        u_next = jnp.where(pos == tiles_per_seq - 1, 0.0, u_next)
        row = lax.broadcasted_iota(jnp.int32, (tm, 1), 0)
        u_m1 = jnp.where(row == 0, u_prev, pltpu.roll(u, 1, 0))
        u_p1 = jnp.where(row == tm - 1, u_next, pltpu.roll(u, tm - 1, 0))
        w = convw_ref[...]
        conv = w[0:1, :] * u_m1 + w[1:2, :] * u + w[2:3, :] * u_p1
        feat_scr[nxt] = (cb_ref[...].astype(F32) * conv).astype(BF16)
        wgu_out_ref[...] = wgu_ref[...].astype(BF16)
        wdn_out_ref[...] = wdn_ref[...].astype(BF16)

    @pl.when(r == 0)
    def _():
        conv_next_tile()

    @pl.when(r > 0)
    def _():
        y_conv = _dot(feat_scr[cur], wconv_ref[...])
        y_mlstm = _dot(hm_ref[...], wml_ref[...])
        merged = (_sigmoid(gc_ref[...].astype(F32)) * y_conv
                  + _sigmoid(gm_ref[...].astype(F32)) * y_mlstm).astype(BF16)
        x1_ref[...] = x_ref[...] + gate1_ref[0] * _dot(merged, wo_ref[...])
        conv_next_tile()


def _mix_call(x2d, main, hm, ada3, conv_w, w_conv_out, w_mlstm_out, w_o, w_gate_up, w_down):
    tm = TM_MIX
    n_tiles = TOKENS // tm
    gu_rows = D_MODEL // n_tiles
    dn_slabs = max(s for s in range(1, n_tiles + 1)
                   if n_tiles % s == 0 and FFN_HIDDEN % (s * BF16_SUBLANES) == 0)
    dn_rows = FFN_HIDDEN // dn_slabs
    dn_steps = n_tiles // dn_slabs
    assert gu_rows % BF16_SUBLANES == 0
    tiles_per_seq = SEQ // tm
    halo = BF16_SUBLANES
    n_halo = TOKENS // halo
    per = tm // halo

    def this_tile(r):
        return jnp.maximum(r - 1, 0)

    def next_tile(r):
        return jnp.minimum(r, n_tiles - 1)

    def col(k):
        return lambda r: (this_tile(r), k)

    def conv_col(k):
        return lambda r: (next_tile(r), k)

    def prev_map(k):
        return lambda r: (jnp.maximum(next_tile(r) * per - 1, 0), k)

    def next_map(k):
        return lambda r: (jnp.minimum((next_tile(r) + 1) * per, n_halo - 1), k)

    def resident(shape):
        return pl.BlockSpec(shape, lambda r: (0, 0), pipeline_mode=pl.Buffered(1))

    def gu_map(r):
        return (next_tile(r), 0)

    def dn_map(r):
        return (next_tile(r) // dn_steps, 0)

    cw = CONV_WIDTH
    return pl.pallas_call(
        _mix_kernel,
        grid=(n_tiles + 1,),
        in_specs=[
            pl.BlockSpec((tm, D_MODEL), col(0)),
            pl.BlockSpec((tm, cw), conv_col(0)),
            pl.BlockSpec((tm, cw), conv_col(1)),
            pl.BlockSpec((tm, cw), conv_col(2)),
            pl.BlockSpec((halo, cw), prev_map(1)),
            pl.BlockSpec((halo, cw), prev_map(2)),
            pl.BlockSpec((halo, cw), next_map(1)),
            pl.BlockSpec((halo, cw), next_map(2)),
            pl.BlockSpec((tm, D_MODEL), col(MAIN_GC // D_MODEL)),
            pl.BlockSpec((tm, D_MODEL), col(MAIN_GM // D_MODEL)),
            pl.BlockSpec((tm, M_V), col(0)),
            pl.BlockSpec((1, 1, D_MODEL),
                         lambda r: ((this_tile(r) // tiles_per_seq) * 6 + 2, 0, 0)),
            resident((3, cw)),
            resident((cw, D_MODEL)),
            resident((M_V, D_MODEL)),
            resident((D_MODEL, D_MODEL)),
            pl.BlockSpec((gu_rows, 2 * FFN_HIDDEN), gu_map),
            pl.BlockSpec((dn_rows, D_MODEL), dn_map),
        ],
        out_specs=[
            pl.BlockSpec((tm, D_MODEL), col(0)),
            pl.BlockSpec((gu_rows, 2 * FFN_HIDDEN), gu_map),
            pl.BlockSpec((dn_rows, D_MODEL), dn_map),
        ],
        out_shape=[
            jax.ShapeDtypeStruct((TOKENS, D_MODEL), F32),
            jax.ShapeDtypeStruct((D_MODEL, 2 * FFN_HIDDEN), BF16),
            jax.ShapeDtypeStruct((FFN_HIDDEN, D_MODEL), BF16),
        ],
        scratch_shapes=[pltpu.VMEM((2, tm, CONV_WIDTH), BF16)],
        compiler_params=pltpu.CompilerParams(
            dimension_semantics=("arbitrary",), vmem_limit_bytes=VMEM_LIMIT),
        name="mix",
    )(x2d, main, main, main, main, main, main, main, main, main, hm, ada3,
      conv_w, w_conv_out, w_mlstm_out, w_o, w_gate_up, w_down)


def _ffn_up_kernel(x1_ref, shift_ref, scale_ref, nw_ref, wg_ref, wu_ref, act_ref, h_scr):
    r = pl.program_id(0)
    f = pl.program_id(1)
    cur = (r + 1) % 2
    nxt = r % 2
    gain = nw_ref[...] * (1.0 + scale_ref[0])
    shift = shift_ref[0]

    def norm_next_rows():
        rows = _next_tile_rows(f, TM_UP, NF_FFN)
        h_scr[nxt, rows, :] = (_rms(x1_ref[rows, :], gain) + shift).astype(BF16)

    @pl.when(r == 0)
    def _():
        norm_next_rows()

    @pl.when(r > 0)
    def _():
        h = h_scr[cur]
        gt = _dot(h, wg_ref[...])
        up = _dot(h, wu_ref[...])
        act_ref[...] = (gt * _sigmoid(gt) * up).astype(BF16)
        norm_next_rows()


def _ffn_up_call(x1, ada3, norm_w, w_gate_up):
    n_tiles = TOKENS // TM_UP
    tiles_per_seq = SEQ // TM_UP

    def next_tile(r):
        return jnp.minimum(r, n_tiles - 1)

    def ada_map(k):
        return lambda r, f: ((next_tile(r) // tiles_per_seq) * 6 + k, 0, 0)

    return pl.pallas_call(
        _ffn_up_kernel,
        grid=(n_tiles + 1, NF_FFN),
        in_specs=[
            pl.BlockSpec((TM_UP, D_MODEL), lambda r, f: (next_tile(r), 0)),
            pl.BlockSpec((1, 1, D_MODEL), ada_map(3)),
            pl.BlockSpec((1, 1, D_MODEL), ada_map(4)),
            pl.BlockSpec((1, D_MODEL), lambda r, f: (0, 0)),
            pl.BlockSpec((D_MODEL, TF_FFN), lambda r, f: (0, f)),
            pl.BlockSpec((D_MODEL, TF_FFN), lambda r, f: (0, NF_FFN + f)),
        ],
        out_specs=pl.BlockSpec((TM_UP, TF_FFN),
                               lambda r, f: (jnp.maximum(r - 1, 0), jnp.where(r > 0, f, 0))),
        out_shape=jax.ShapeDtypeStruct((TOKENS, FFN_HIDDEN), BF16),
        scratch_shapes=[pltpu.VMEM((2, TM_UP, D_MODEL), BF16)],
        compiler_params=pltpu.CompilerParams(
            dimension_semantics=("arbitrary", "arbitrary"), vmem_limit_bytes=VMEM_LIMIT),
        name="ffn_up",
    )(x1, ada3, ada3, norm_w, w_gate_up, w_gate_up)


def _ffn_down_kernel(act_ref, x1_ref, gate_ref, fnw_ref, wd_ref, out_ref):
    x2 = x1_ref[...] + gate_ref[0] * _dot(act_ref[...], wd_ref[...])
    out_ref[...] = _rms(x2, fnw_ref[...])


def _ffn_down_call(act, x1, ada3, final_norm_w, w_down):
    tm = TM_DOWN
    tiles_per_seq = SEQ // tm
    return pl.pallas_call(
        _ffn_down_kernel,
        grid=(TOKENS // tm,),
        in_specs=[
            pl.BlockSpec((tm, FFN_HIDDEN), lambda i: (i, 0)),
            pl.BlockSpec((tm, D_MODEL), lambda i: (i, 0)),
            pl.BlockSpec((1, 1, D_MODEL), lambda i: ((i // tiles_per_seq) * 6 + 5, 0, 0)),
            pl.BlockSpec((1, D_MODEL), lambda i: (0, 0)),
            pl.BlockSpec((FFN_HIDDEN, D_MODEL), lambda i: (0, 0), pipeline_mode=pl.Buffered(1)),
        ],
        out_specs=pl.BlockSpec((tm, D_MODEL), lambda i: (i, 0)),
        out_shape=jax.ShapeDtypeStruct((TOKENS, D_MODEL), F32),
        compiler_params=pltpu.CompilerParams(
            dimension_semantics=("arbitrary",), vmem_limit_bytes=VMEM_LIMIT),
        name="ffn_down",
    )(act, x1, ada3, final_norm_w, w_down)


def kernel(x, c, w_ada, b_ada, norm1_w, w_in_mix, conv_w, mlstm_gate_bias, mlstm_norm_w,
           w_conv_out, w_mlstm_out, w_o, norm2_w, w_gate_up, w_down, final_norm_w):
    assert x.shape == (BATCH, SEQ, D_MODEL) and w_ada.shape[0] == 1
    x2d = x.reshape(TOKENS, D_MODEL)

    ada = _ada_call(c, w_ada[0], b_ada[0])
    ada3 = ada.reshape(BATCH * 6, 1, D_MODEL)

    w_in = w_in_mix[0]
    w_all = lax.dynamic_update_slice(w_in.astype(BF16), w_in[:, OFF_BG:].astype(BF16), (0, OFF_G))
    w_gate = jnp.pad(w_in[:, OFF_G:OFF_BG], ((0, 0), (0, LANES - N_GATES))).astype(BF16)
    b_gate = jnp.pad(mlstm_gate_bias[0], (0, LANES - N_GATES)).reshape(1, LANES)

    main, kt, gpre = _inproj_call(x2d, ada3, norm1_w[0].reshape(1, D_MODEL),
                                  w_all, w_gate, b_gate)

    gates = gpre[:, :N_GATES].reshape(BATCH, N_CHUNKS, CHUNK, 4, M_HEADS)
    gates = gates.transpose(0, 4, 3, 1, 2)
    hm = _mlstm_call(gates, main.reshape(BATCH, SEQ, MAIN_COLS), kt,
                     mlstm_norm_w[0].reshape(1, M_V))

    x1, w_gate_up_bf16, w_down_bf16 = _mix_call(
        x2d, main, hm.reshape(TOKENS, M_V), ada3, conv_w[0], w_conv_out[0].astype(BF16),
        w_mlstm_out[0].astype(BF16), w_o[0].astype(BF16), w_gate_up[0], w_down[0])

    act = _ffn_up_call(x1, ada3, norm2_w[0].reshape(1, D_MODEL), w_gate_up_bf16)
    out = _ffn_down_call(act, x1, ada3, final_norm_w.reshape(1, D_MODEL), w_down_bf16)
    return out.reshape(BATCH, SEQ, D_MODEL)
```
